```python
import jax, jax.numpy as jnp
from jax import lax
import numpy as np

D_MODEL = 1024
BATCH = 8
SEQ = 2048
DEPTH = 2
DEC_BATCH = 128
DEC_SEQ = 8
PAST_LEN = 16384
PAGE_SIZE = 128

SSD_HEAD_DIM = 64
SSD_INNER = D_MODEL
SSD_HEADS = SSD_INNER // SSD_HEAD_DIM
SSD_GROUPS = 4
SSD_STATE = 128
SSD_CONV = 4
SSD_CHUNK = 128
D_XBC = SSD_INNER + 2 * SSD_GROUPS * SSD_STATE
D_SC = D_MODEL
SC_CONV = 3
D_FF = 2816
N_MOD = 9
EPS = 1e-6
D_PROJ = SSD_INNER + D_XBC + SSD_HEADS + 3 * D_SC + 2 * D_MODEL

kernel_name = "hybrid_ssd_shortconv_adaln_decoder_step"


def _split(t, sizes):
    idx = np.cumsum(sizes)[:-1].tolist()
    return jnp.split(t, idx, axis=-1)


def rmsnorm(x, g):
    xf = x.astype(jnp.float32)
    r = lax.rsqrt(jnp.mean(xf * xf, axis=-1, keepdims=True) + EPS)
    return (xf * r).astype(x.dtype) * g


def group_rmsnorm(x, g, n_groups):
    shp = x.shape
    xf = x.astype(jnp.float32).reshape(shp[:-1] + (n_groups, shp[-1] // n_groups))
    r = lax.rsqrt(jnp.mean(xf * xf, axis=-1, keepdims=True) + EPS)
    return (xf * r).reshape(shp).astype(x.dtype) * g


def modulate(h, shift, scale):
    return h * (1.0 + scale[:, None, :]) + shift[:, None, :]


def causal_dwconv(u, buf, w):
    k_w = w.shape[0]
    L = u.shape[1]
    ext = jnp.concatenate([buf.astype(u.dtype), u], axis=1)
    y = ext[:, 0:L] * w[0]
    for k in range(1, k_w):
        y = y + ext[:, k:k + L] * w[k]
    return y, ext[:, L:]


def ssd_scan(xh, dt, a, bm, cm, s0):
    bsz, L, H, P = xh.shape
    G, N = bm.shape[2], bm.shape[3]
    R = H // G
    T = SSD_CHUNK if L >= SSD_CHUNK else L
    pad = (-L) % T
    f32 = jnp.float32
    xh, dt, bm, cm = xh.astype(f32), dt.astype(f32), bm.astype(f32), cm.astype(f32)
    if pad:
        pw = ((0, 0), (0, pad), (0, 0))
        xh = jnp.pad(xh, pw + ((0, 0),))
        dt = jnp.pad(dt, pw)
        bm = jnp.pad(bm, pw + ((0, 0),))
        cm = jnp.pad(cm, pw + ((0, 0),))
    nc = (L + pad) // T
    x = xh.reshape(bsz, nc, T, G, R, P)
    dtc = dt.reshape(bsz, nc, T, G, R)
    b_ = bm.reshape(bsz, nc, T, G, N)
    c_ = cm.reshape(bsz, nc, T, G, N)
    xdt = x * dtc[..., None]
    da = jnp.transpose(dtc * a.astype(f32).reshape(G, R), (0, 1, 3, 4, 2))
    acum = jnp.cumsum(da, axis=-1)
    seg = acum[..., :, None] - acum[..., None, :]
    mask = jnp.tril(jnp.ones((T, T), dtype=bool))
    decay = jnp.exp(jnp.where(mask, seg, -jnp.inf))
    cb = jnp.einsum('bclgn,bcsgn->bcgls', c_, b_)
    y_diag = jnp.einsum('bcgls,bcgrls,bcsgrp->bclgrp', cb, decay, xdt)
    decay_states = jnp.exp(acum[..., -1:] - acum)
    states = jnp.einsum('bclgn,bcgrl,bclgrp->bcgrpn', b_, decay_states, xdt)
    chunk_decay = jnp.exp(acum[..., -1])

    def step(s, inp):
        st, dec = inp
        return dec[..., None, None] * s + st, s

    s_final, s_in = lax.scan(step, s0.astype(f32).reshape(bsz, G, R, P, N),
                             (jnp.moveaxis(states, 1, 0), jnp.moveaxis(chunk_decay, 1, 0)))
    s_in = jnp.moveaxis(s_in, 0, 1)
    y_off = jnp.einsum('bclgn,bcgrpn,bcgrl->bclgrp', c_, s_in, jnp.exp(acum))
    y = (y_diag + y_off).reshape(bsz, nc * T, H, P)[:, :L]
    return y, s_final.reshape(bsz, H, P, N)


def swiglu(h, w_gu, w_down):
    g, u = jnp.split(h @ w_gu, 2, axis=-1)
    return (jax.nn.silu(g) * u) @ w_down


def trunk_layer(x, c, s_ssm, buf_xbc, buf_sc,
                w_ada, b_ada, norm_ffn1, norm_mix, norm_ffn2,
                ffn1_w_gu, ffn1_w_down, ffn2_w_gu, ffn2_w_down,
                w_in, ssd_conv_w, ssd_conv_b, ssd_dt_bias, ssd_a_log, ssd_d, ssd_norm,
                w_out_ssd, sc_conv_w, w_out_sc, w_o):
    bsz, L, _ = x.shape
    mod = jax.nn.silu(c) @ w_ada + b_ada
    sh1, sc1, g1, sh2, sc2, g2, sh3, sc3, g3 = jnp.split(mod, N_MOD, axis=-1)

    h = modulate(rmsnorm(x, norm_ffn1), sh1, sc1)
    x = x + 0.5 * g1[:, None, :] * swiglu(h, ffn1_w_gu, ffn1_w_down)

    h = modulate(rmsnorm(x, norm_mix), sh2, sc2)
    proj = h @ w_in
    z, xbc, dt_raw, sc_b, sc_c, sc_h, gate_ssd, gate_sc = _split(
        proj, [SSD_INNER, D_XBC, SSD_HEADS, D_SC, D_SC, D_SC, D_MODEL, D_MODEL])

    xbc_c, new_buf_xbc = causal_dwconv(xbc, buf_xbc, ssd_conv_w)
    xbc_c = jax.nn.silu(xbc_c + ssd_conv_b)
    xs, bs, cs = _split(xbc_c, [SSD_INNER, SSD_GROUPS * SSD_STATE, SSD_GROUPS * SSD_STATE])
    dt = jax.nn.softplus(dt_raw.astype(jnp.float32) + ssd_dt_bias.astype(jnp.float32))
    a = -jnp.exp(ssd_a_log.astype(jnp.float32))
    xh = xs.reshape(bsz, L, SSD_HEADS, SSD_HEAD_DIM)
    y, s_new = ssd_scan(xh, dt, a,
                        bs.reshape(bsz, L, SSD_GROUPS, SSD_STATE),
                        cs.reshape(bsz, L, SSD_GROUPS, SSD_STATE), s_ssm)
    y = y.astype(x.dtype) + xh * ssd_d[:, None]
    y = y.reshape(bsz, L, SSD_INNER) * jax.nn.silu(z)
    y = group_rmsnorm(y, ssd_norm, SSD_GROUPS)
    y_ssd = y @ w_out_ssd

    u, new_buf_sc = causal_dwconv(sc_c * sc_h, buf_sc, sc_conv_w)
    y_sc = (sc_b * u) @ w_out_sc

    merged = jax.nn.sigmoid(gate_ssd) * y_ssd + jax.nn.sigmoid(gate_sc) * y_sc
    x = x + g2[:, None, :] * (merged @ w_o)

    h = modulate(rmsnorm(x, norm_ffn2), sh3, sc3)
    x = x + 0.5 * g3[:, None, :] * swiglu(h, ffn2_w_gu, ffn2_w_down)
    return x, s_new.astype(s_ssm.dtype), new_buf_xbc, new_buf_sc


def setup_inputs(seed: int = 0) -> dict:
    key = jax.random.key(seed)
    ks = iter(jax.random.split(key, 40))
    nrm = lambda shape, s: jax.random.normal(next(ks), shape, jnp.float32) * s
    gain = lambda shape: 1.0 + nrm(shape, 0.02)
    dt0 = jnp.exp(jax.random.uniform(next(ks), (DEPTH, SSD_HEADS), jnp.float32,
                                     np.log(1e-3), np.log(1e-1)))
    a0 = jax.random.uniform(next(ks), (DEPTH, SSD_HEADS), jnp.float32, 1.0, 16.0)
    return {
        "x_prompt": nrm((BATCH, SEQ, D_MODEL), 1.0),
        "x_sample": nrm((DEC_BATCH, DEC_SEQ, D_MODEL), 1.0),
        "c_prompt": nrm((BATCH, D_MODEL), 1.0),
        "c_sample": nrm((DEC_BATCH, D_MODEL), 1.0),
        "state_ssm": nrm((DEPTH, DEC_BATCH, SSD_HEADS, SSD_HEAD_DIM, SSD_STATE), 0.3),
        "state_conv_ssd": nrm((DEPTH, DEC_BATCH, SSD_CONV - 1, D_XBC), 1.0),
        "state_conv_short": nrm((DEPTH, DEC_BATCH, SC_CONV - 1, D_SC), 1.0),
        "w_ada": nrm((DEPTH, D_MODEL, N_MOD * D_MODEL), 0.5 * D_MODEL ** -0.5),
        "b_ada": nrm((DEPTH, N_MOD * D_MODEL), 0.02),
        "norm_ffn1": gain((DEPTH, D_MODEL)),
        "norm_mix": gain((DEPTH, D_MODEL)),
        "norm_ffn2": gain((DEPTH, D_MODEL)),
        "ffn1_w_gu": nrm((DEPTH, D_MODEL, 2 * D_FF), D_MODEL ** -0.5),
        "ffn1_w_down": nrm((DEPTH, D_FF, D_MODEL), D_FF ** -0.5),
        "ffn2_w_gu": nrm((DEPTH, D_MODEL, 2 * D_FF), D_MODEL ** -0.5),
        "ffn2_w_down": nrm((DEPTH, D_FF, D_MODEL), D_FF ** -0.5),
        "w_in": nrm((DEPTH, D_MODEL, D_PROJ), D_MODEL ** -0.5),
        "ssd_conv_w": nrm((DEPTH, SSD_CONV, D_XBC), SSD_CONV ** -0.5),
        "ssd_conv_b": nrm((DEPTH, D_XBC), 0.02),
        "ssd_dt_bias": dt0 + jnp.log(-jnp.expm1(-dt0)),
        "ssd_a_log": jnp.log(a0),
        "ssd_d": gain((DEPTH, SSD_HEADS)),
        "ssd_norm": gain((DEPTH, SSD_INNER)),
        "w_out_ssd": nrm((DEPTH, SSD_INNER, D_MODEL), SSD_INNER ** -0.5),
        "sc_conv_w": nrm((DEPTH, SC_CONV, D_SC), SC_CONV ** -0.5),
        "w_out_sc": nrm((DEPTH, D_SC, D_MODEL), D_SC ** -0.5),
        "w_o": nrm((DEPTH, D_MODEL, D_MODEL), D_MODEL ** -0.5),
        "norm_final": gain((D_MODEL,)),
    }


def reference(x_prompt, x_sample, c_prompt, c_sample, state_ssm, state_conv_ssd,
              state_conv_short, w_ada, b_ada, norm_ffn1, norm_mix, norm_ffn2,
              ffn1_w_gu, ffn1_w_down, ffn2_w_gu, ffn2_w_down, w_in, ssd_conv_w,
              ssd_conv_b, ssd_dt_bias, ssd_a_log, ssd_d, ssd_norm, w_out_ssd,
              sc_conv_w, w_out_sc, w_o, norm_final):
    bp = x_prompt.shape[0]
    dtp = x_prompt.dtype
    xp, xs = x_prompt, x_sample
    ssm_p, cx_p, cs_p, ssm_s, cx_s, cs_s = [], [], [], [], [], []
    for l in range(DEPTH):
        lw = (w_ada[l], b_ada[l], norm_ffn1[l], norm_mix[l], norm_ffn2[l],
              ffn1_w_gu[l], ffn1_w_down[l], ffn2_w_gu[l], ffn2_w_down[l],
              w_in[l], ssd_conv_w[l], ssd_conv_b[l], ssd_dt_bias[l], ssd_a_log[l],
              ssd_d[l], ssd_norm[l], w_out_ssd[l], sc_conv_w[l], w_out_sc[l], w_o[l])
        xp, s1, b1, b2 = trunk_layer(
            xp, c_prompt,
            jnp.zeros((bp, SSD_HEADS, SSD_HEAD_DIM, SSD_STATE), dtp),
            jnp.zeros((bp, SSD_CONV - 1, D_XBC), dtp),
            jnp.zeros((bp, SC_CONV - 1, D_SC), dtp), *lw)
        ssm_p.append(s1); cx_p.append(b1); cs_p.append(b2)
        xs, s1, b1, b2 = trunk_layer(
            xs, c_sample, state_ssm[l], state_conv_ssd[l], state_conv_short[l], *lw)
        ssm_s.append(s1); cx_s.append(b1); cs_s.append(b2)
    y_prompt = rmsnorm(xp, norm_final)
    y_sample = rmsnorm(xs, norm_final)
    return (y_prompt, y_sample,
            jnp.stack(ssm_p), jnp.stack(cx_p), jnp.stack(cs_p),
            jnp.stack(ssm_s), jnp.stack(cx_s), jnp.stack(cs_s))
```

```python
import functools

import jax
import jax.numpy as jnp
from jax import lax
from jax.experimental import pallas as pl
from jax.experimental.pallas import tpu as pltpu

F32 = jnp.float32
BF16 = jnp.bfloat16

D_MODEL = 1024
SSD_HEADS = 16
SSD_HEAD_DIM = 64
SSD_GROUPS = 4
SSD_STATE = 128
HEADS_PER_GROUP = SSD_HEADS // SSD_GROUPS
GROUP_WIDTH = HEADS_PER_GROUP * SSD_HEAD_DIM
SSD_INNER = SSD_HEADS * SSD_HEAD_DIM
D_XBC = SSD_INNER + 2 * SSD_GROUPS * SSD_STATE
SSD_CONV = 4
SC_CONV = 3
D_FF = 2816
N_MOD = 9
EPS = 1e-6

LANES = 128
SUBLANES = 8
HALO = SUBLANES
DT_PAD = LANES

C_Z = 0
C_XBC = C_Z + SSD_INNER
C_SCB = C_XBC + D_XBC
C_SCC = C_SCB + D_MODEL
C_SCH = C_SCC + D_MODEL
C_GSSD = C_SCH + D_MODEL
C_GSC = C_GSSD + D_MODEL
C_DT = C_GSC + D_MODEL
D_PROJ_PAD = C_DT + DT_PAD

FF_CHUNK = 256
FFN_ROWS = 512
MIX_ROWS_PROMPT = 256
MIX_BATCH_SAMPLE = 8
SSD_CHUNK = 128
VMEM_LIMIT = 56 * 1024 * 1024


def _dot(a, b):
    return jnp.dot(a, b, preferred_element_type=F32)


def _dot_nt(a, b):
    return lax.dot_general(a, b, (((1,), (1,)), ((), ())), preferred_element_type=F32)


def _dot_tn(a, b):
    return lax.dot_general(a, b, (((0,), (0,)), ((), ())), preferred_element_type=F32)


def _split3(x):
    hi = x.astype(BF16)
    r1 = x - hi.astype(F32)
    mid = r1.astype(BF16)
    lo = (r1 - mid.astype(F32)).astype(BF16)
    return hi, mid, lo


def _dot_exact_rhs(x, sel):
    hi, mid, lo = _split3(x)
    return _dot(hi, sel) + _dot(mid, sel) + _dot(lo, sel)


def _dot_exact_lhs(sel, x):
    hi, mid, lo = _split3(x)
    return _dot(sel, hi) + _dot(sel, mid) + _dot(sel, lo)


def _silu(x):
    return x / (1.0 + jnp.exp(-x))


def _sigmoid(x):
    return 1.0 / (1.0 + jnp.exp(-x))


def _softplus(x):
    return jnp.maximum(x, 0.0) + jnp.log1p(jnp.exp(-jnp.abs(x)))


def _rmsnorm(x, g):
    r = lax.rsqrt(jnp.mean(x * x, axis=-1, keepdims=True) + EPS)
    return (x * r) * g


def _const_spec(shape):
    nd = len(shape)
    return pl.BlockSpec(shape, lambda *_: (0,) * nd, pipeline_mode=pl.Buffered(1))


def _mod_body(c_ref, w_ref, b_ref, o_ref):
    sc = _silu(c_ref[...]).astype(BF16)
    o_ref[0] = _dot(sc, w_ref[0].astype(BF16)) + b_ref[0]


def _mod_call(c_all, w_ada, b_ada):
    depth, d, n = w_ada.shape
    m = c_all.shape[0]
    tn = D_MODEL
    return pl.pallas_call(
        _mod_body,
        grid=(depth, n // tn),
        in_specs=[
            pl.BlockSpec((m, d), lambda l, j: (0, 0)),
            pl.BlockSpec((1, d, tn), lambda l, j: (l, 0, j)),
            pl.BlockSpec((1, 1, tn), lambda l, j: (l, 0, j)),
        ],
        out_specs=pl.BlockSpec((1, m, tn), lambda l, j: (l, 0, j)),
        out_shape=jax.ShapeDtypeStruct((depth, m, n), F32),
        compiler_params=pltpu.CompilerParams(
            dimension_semantics=("arbitrary", "arbitrary"),
            vmem_limit_bytes=VMEM_LIMIT),
        name="adaln_mod",
    )(c_all, w_ada, b_ada.reshape(depth, 1, n))


def _ffn_body(x_ref, sh_ref, sc_ref, gt_ref, nw_ref, wgu_ref, wd_ref, nf_ref,
              o_ref, act_ref, *, final_norm):
    x = x_ref[...]
    nb, lb, d = x.shape
    h = _rmsnorm(x, nw_ref[...]) * (1.0 + sc_ref[...]) + sh_ref[...]
    hb = h.reshape(nb * lb, d).astype(BF16)
    for c in range(D_FF // FF_CHUNK):
        lo = c * FF_CHUNK
        g = _dot(hb, wgu_ref[:, lo:lo + FF_CHUNK])
        u = _dot(hb, wgu_ref[:, D_FF + lo:D_FF + lo + FF_CHUNK])
        act_ref[:, lo:lo + FF_CHUNK] = (_silu(g) * u).astype(BF16)
    y = _dot(act_ref[...], wd_ref[...])
    out = x + (0.5 * gt_ref[...]) * y.reshape(nb, lb, d)
    if final_norm:
        out = _rmsnorm(out, nf_ref[...])
    o_ref[...] = out


def _ffn_call(x, mod, k_mod, norm_w, w_gu, w_down, norm_final, *, nb, lb, final_norm):
    NB, LB, d = x.shape
    rows = nb * lb

    def mod_spec(k):
        return pl.BlockSpec((nb, 1, d), lambda b, i: (b, 0, k))

    return pl.pallas_call(
        functools.partial(_ffn_body, final_norm=final_norm),
        grid=(NB // nb, LB // lb),
        in_specs=[
            pl.BlockSpec((nb, lb, d), lambda b, i: (b, i, 0)),
            mod_spec(k_mod), mod_spec(k_mod + 1), mod_spec(k_mod + 2),
            _const_spec((1, d)),
            _const_spec(w_gu.shape),
            _const_spec(w_down.shape),
            _const_spec((1, d)),
        ],
        out_specs=pl.BlockSpec((nb, lb, d), lambda b, i: (b, i, 0)),
        out_shape=jax.ShapeDtypeStruct(x.shape, F32),
        scratch_shapes=[pltpu.VMEM((rows, D_FF), BF16)],
        compiler_params=pltpu.CompilerParams(
            dimension_semantics=("arbitrary", "arbitrary"),
            vmem_limit_bytes=VMEM_LIMIT),
        name="ffn",
    )(x, mod, mod, mod, norm_w.reshape(1, d), w_gu, w_down, norm_final.reshape(1, d))


def _causal_conv(ext_ref, cur, w_ref, new_buf_ref, lb, k_w):
    ext_ref[:, HALO:HALO + lb, :] = cur
    first = HALO - (k_w - 1)
    acc = None
    for k in range(k_w):
        tap = ext_ref[:, first + k:first + k + lb, :] * w_ref[k:k + 1, :]
        acc = tap if acc is None else acc + tap
    new_buf_ref[...] = ext_ref[:, lb + first:lb + HALO, :]
    ext_ref[:, 0:HALO, :] = ext_ref[:, lb:lb + HALO, :]
    return acc


def _mixer_body(x_ref, sh_ref, sc_ref, gt_ref, sin_ref, bxin_ref, bsin_ref,
                nw_ref, win_ref, cw_ref, cb_ref, dtb_ref, alog_ref, dexp_ref, snorm_ref,
                wos_ref, scw_ref, wosc_ref, wo_ref,
                o_ref, sout_ref, bxout_ref, bsout_ref,
                extx_ref, exts_ref, y_ref, *, nb, lb, chunk):
    rows = nb * lb
    seg = min(lb, chunk)
    nseg = chunk // seg
    seg_shift = seg.bit_length() - 1
    assert seg == 1 << seg_shift and rows % chunk == 0

    @pl.when(pl.program_id(1) == 0)
    def _():
        sout_ref[...] = sin_ref[...]
        extx_ref[:, HALO - (SSD_CONV - 1):HALO, :] = bxin_ref[...]
        exts_ref[:, HALO - (SC_CONV - 1):HALO, :] = bsin_ref[...]

    x = x_ref[...]
    h = _rmsnorm(x, nw_ref[...]) * (1.0 + sc_ref[...]) + sh_ref[...]
    hb = h.reshape(rows, D_MODEL).astype(BF16)

    def proj(lo, width):
        return _dot(hb, win_ref[:, lo:lo + width])

    xbc = proj(C_XBC, D_XBC).reshape(nb, lb, D_XBC)
    xbc = _causal_conv(extx_ref, xbc, cw_ref, bxout_ref, lb, SSD_CONV)
    xbc = _silu(xbc + cb_ref[...]).reshape(rows, D_XBC)
    xs_all = xbc[:, 0:SSD_INNER]

    dt_all = _softplus(proj(C_DT, DT_PAD) + dtb_ref[...])
    da_all = dt_all * (-jnp.exp(alog_ref[...]))

    ri = lax.broadcasted_iota(jnp.int32, (chunk, chunk), 0)
    ci = lax.broadcasted_iota(jnp.int32, (chunk, chunk), 1)
    same_seq = (ri >> seg_shift) == (ci >> seg_shift)
    causal = jnp.logical_and(same_seq, ci <= ri)
    causal_sel = causal.astype(F32).astype(BF16)
    seq_sel = same_seq.astype(F32).astype(BF16)
    eh = lax.broadcasted_iota(jnp.int32, (LANES, SSD_INNER), 0)
    ej = lax.broadcasted_iota(jnp.int32, (LANES, SSD_INNER), 1)
    head_sel = (eh == ej // SSD_HEAD_DIM).astype(F32).astype(BF16)
    lane_head = lax.broadcasted_iota(jnp.int32, (chunk, GROUP_WIDTH), 1) // SSD_HEAD_DIM

    for c in range(rows // chunk):
        r0 = c * chunk
        xs = xbc[r0:r0 + chunk, 0:SSD_INNER]
        bm = xbc[r0:r0 + chunk, SSD_INNER:SSD_INNER + SSD_GROUPS * SSD_STATE]
        cm = xbc[r0:r0 + chunk, SSD_INNER + SSD_GROUPS * SSD_STATE:D_XBC]
        dt = dt_all[r0:r0 + chunk]
        da = da_all[r0:r0 + chunk]
        acum = _dot_exact_lhs(causal_sel, da)
        atot = _dot_exact_lhs(seq_sel, da)
        ex = _dot_exact_rhs(jnp.concatenate([dt, acum, atot - acum], axis=0), head_sel)
        dt_e = ex[0:chunk]
        a_e = ex[chunk:2 * chunk]
        dl_e = ex[2 * chunk:3 * chunk]
        xdt = xs * dt_e
        xdt_b = xdt.astype(BF16)
        xd = xdt * jnp.exp(dl_e)
        e_a = jnp.exp(a_e)
        acum_t = acum.T

        for g in range(SSD_GROUPS):
            gl = g * GROUP_WIDTH
            bg = bm[:, g * SSD_STATE:(g + 1) * SSD_STATE]
            cg = cm[:, g * SSD_STATE:(g + 1) * SSD_STATE]
            cbm = _dot_nt(cg.astype(BF16), bg.astype(BF16))
            ms = []
            for r in range(HEADS_PER_GROUP):
                hh = g * HEADS_PER_GROUP + r
                sgm = acum[:, hh:hh + 1] - acum_t[hh:hh + 1, :]
                dec = jnp.exp(jnp.where(causal, sgm, -jnp.inf))
                ms.append((cbm * dec).astype(BF16))
            mg = jnp.concatenate(ms, axis=1)
            xg = xdt_b[:, gl:gl + GROUP_WIDTH]
            rhs = jnp.concatenate(
                [jnp.where(lane_head == r, xg, jnp.zeros_like(xg))
                 for r in range(HEADS_PER_GROUP)], axis=0)
            y_diag = _dot(mg, rhs)

            for j in range(nseg):
                q0 = j * seg
                b_loc = (r0 + q0) // lb
                s_old = sout_ref[b_loc, gl:gl + GROUP_WIDTH, :]
                y_off = _dot_nt(cg[q0:q0 + seg].astype(BF16), s_old.astype(BF16))
                y_ref[r0 + q0:r0 + q0 + seg, gl:gl + GROUP_WIDTH] = (
                    y_diag[q0:q0 + seg] + y_off * e_a[q0:q0 + seg, gl:gl + GROUP_WIDTH])
                s_add = _dot_tn(xd[q0:q0 + seg, gl:gl + GROUP_WIDTH].astype(BF16),
                                bg[q0:q0 + seg].astype(BF16))
                q_last = q0 + seg - 1
                for r in range(HEADS_PER_GROUP):
                    hh = g * HEADS_PER_GROUP + r
                    keep = jnp.exp(atot[q_last:q_last + 1, hh:hh + 1])
                    p0 = r * SSD_HEAD_DIM
                    sout_ref[b_loc, gl + p0:gl + p0 + SSD_HEAD_DIM, :] = (
                        keep * s_old[p0:p0 + SSD_HEAD_DIM] + s_add[p0:p0 + SSD_HEAD_DIM])

    y = y_ref[...] + xs_all * dexp_ref[...]
    y = y * _silu(proj(C_Z, SSD_INNER))
    parts = []
    for g in range(SSD_GROUPS):
        yg = y[:, g * GROUP_WIDTH:(g + 1) * GROUP_WIDTH]
        parts.append(yg * lax.rsqrt(jnp.mean(yg * yg, axis=-1, keepdims=True) + EPS))
    yn = jnp.concatenate(parts, axis=1) * snorm_ref[...]
    y_ssd = _dot(yn.astype(BF16), wos_ref[...])

    v = (proj(C_SCC, D_MODEL) * proj(C_SCH, D_MODEL)).reshape(nb, lb, D_MODEL)
    u = _causal_conv(exts_ref, v, scw_ref, bsout_ref, lb, SC_CONV).reshape(rows, D_MODEL)
    y_sc = _dot((proj(C_SCB, D_MODEL) * u).astype(BF16), wosc_ref[...])

    merged = (_sigmoid(proj(C_GSSD, D_MODEL)) * y_ssd
              + _sigmoid(proj(C_GSC, D_MODEL)) * y_sc)
    o_ref[...] = x + gt_ref[...] * _dot(merged.astype(BF16), wo_ref[...]).reshape(nb, lb, D_MODEL)


def _mixer_call(x, mod, s_in, bx_in, bs_in, lw, *, nb, lb, chunk):
    NB, LB, d = x.shape
    rows = nb * lb

    def mod_spec(k):
        return pl.BlockSpec((nb, 1, d), lambda b, i: (b, 0, k))

    def per_seq(shape):
        return pl.BlockSpec((nb,) + shape, lambda b, i: (b, 0, 0))

    consts = [lw["norm_mix"], lw["w_in"], lw["ssd_conv_w"], lw["ssd_conv_b"], lw["dt_bias"],
              lw["a_log"], lw["d_exp"], lw["ssd_norm"], lw["w_out_ssd"], lw["sc_conv_w"],
              lw["w_out_sc"], lw["w_o"]]
    return pl.pallas_call(
        functools.partial(_mixer_body, nb=nb, lb=lb, chunk=chunk),
        grid=(NB // nb, LB // lb),
        in_specs=[
            pl.BlockSpec((nb, lb, d), lambda b, i: (b, i, 0)),
            mod_spec(3), mod_spec(4), mod_spec(5),
            per_seq(s_in.shape[1:]), per_seq(bx_in.shape[1:]), per_seq(bs_in.shape[1:]),
        ] + [_const_spec(w.shape) for w in consts],
        out_specs=[
            pl.BlockSpec((nb, lb, d), lambda b, i: (b, i, 0)),
            per_seq(s_in.shape[1:]), per_seq(bx_in.shape[1:]), per_seq(bs_in.shape[1:]),
        ],
        out_shape=[
            jax.ShapeDtypeStruct(x.shape, F32),
            jax.ShapeDtypeStruct(s_in.shape, F32),
            jax.ShapeDtypeStruct(bx_in.shape, F32),
            jax.ShapeDtypeStruct(bs_in.shape, F32),
        ],
        scratch_shapes=[
            pltpu.VMEM((nb, HALO + lb, D_XBC), F32),
            pltpu.VMEM((nb, HALO + lb, D_MODEL), F32),
            pltpu.VMEM((rows, SSD_INNER), F32),
        ],
        compiler_params=pltpu.CompilerParams(
            dimension_semantics=("arbitrary", "arbitrary"),
            vmem_limit_bytes=VMEM_LIMIT),
        name="mixer",
    )(x, mod, mod, mod, s_in, bx_in, bs_in, *consts)


def _layer_weights(l, w_in, norm_mix, ssd_conv_w, ssd_conv_b, ssd_dt_bias, ssd_a_log, ssd_d,
                   ssd_norm, w_out_ssd, sc_conv_w, w_out_sc, w_o):
    d = D_MODEL
    w = w_in[l]
    dt_lo = SSD_INNER + D_XBC
    dt_hi = dt_lo + SSD_HEADS
    w_re = jnp.concatenate(
        [w[:, :dt_lo], w[:, dt_hi:], w[:, dt_lo:dt_hi],
         jnp.zeros((d, DT_PAD - SSD_HEADS), w.dtype)], axis=1).astype(BF16)
    pad = lambda v: jnp.pad(v, (0, DT_PAD - SSD_HEADS)).reshape(1, DT_PAD)
    return {
        "norm_mix": norm_mix[l].reshape(1, d),
        "w_in": w_re,
        "ssd_conv_w": ssd_conv_w[l],
        "ssd_conv_b": ssd_conv_b[l].reshape(1, D_XBC),
        "dt_bias": pad(ssd_dt_bias[l]),
        "a_log": pad(ssd_a_log[l]),
        "d_exp": jnp.repeat(ssd_d[l], SSD_HEAD_DIM).reshape(1, SSD_INNER),
        "ssd_norm": ssd_norm[l].reshape(1, SSD_INNER),
        "w_out_ssd": w_out_ssd[l].astype(BF16),
        "sc_conv_w": sc_conv_w[l],
        "w_out_sc": w_out_sc[l].astype(BF16),
        "w_o": w_o[l].astype(BF16),
    }


def kernel(x_prompt, x_sample, c_prompt, c_sample, state_ssm, state_conv_ssd, state_conv_short, w_ada, b_ada, norm_ffn1, norm_mix, norm_ffn2, ffn1_w_gu, ffn1_w_down, ffn2_w_gu, ffn2_w_down, w_in, ssd_conv_w, ssd_conv_b, ssd_dt_bias, ssd_a_log, ssd_d, ssd_norm, w_out_ssd, sc_conv_w, w_out_sc, w_o, norm_final):
    depth = w_ada.shape[0]
    bp, lp, d = x_prompt.shape
    bs, ls, _ = x_sample.shape
    hp = SSD_HEADS * SSD_HEAD_DIM

    mod = _mod_call(jnp.concatenate([c_prompt, c_sample], axis=0), w_ada, b_ada)
    zeros_s = jnp.zeros((bp, hp, SSD_STATE), F32)
    zeros_bx = jnp.zeros((bp, SSD_CONV - 1, D_XBC), F32)
    zeros_bs = jnp.zeros((bp, SC_CONV - 1, D_MODEL), F32)

    ffn_p = dict(nb=1, lb=min(FFN_ROWS, lp))
    ffn_s = dict(nb=min(FFN_ROWS // ls, bs), lb=ls)
    mix_p = dict(nb=1, lb=min(MIX_ROWS_PROMPT, lp), chunk=min(SSD_CHUNK, lp))
    mix_s = dict(nb=min(MIX_BATCH_SAMPLE, bs), lb=ls, chunk=min(MIX_BATCH_SAMPLE, bs) * ls)

    xp, xs = x_prompt, x_sample
    outs = [[] for _ in range(6)]
    for l in range(depth):
        last = l == depth - 1
        mod_p = mod[l, :bp].reshape(bp, 1, N_MOD * d)
        mod_s = mod[l, bp:].reshape(bs, 1, N_MOD * d)
        gu1, dn1 = ffn1_w_gu[l].astype(BF16), ffn1_w_down[l].astype(BF16)
        gu2, dn2 = ffn2_w_gu[l].astype(BF16), ffn2_w_down[l].astype(BF16)
        lw = _layer_weights(l, w_in, norm_mix, ssd_conv_w, ssd_conv_b, ssd_dt_bias, ssd_a_log,
                            ssd_d, ssd_norm, w_out_ssd, sc_conv_w, w_out_sc, w_o)

        xp = _ffn_call(xp, mod_p, 0, norm_ffn1[l], gu1, dn1, norm_final, final_norm=False, **ffn_p)
        xs = _ffn_call(xs, mod_s, 0, norm_ffn1[l], gu1, dn1, norm_final, final_norm=False, **ffn_s)

        xp, s_p, bx_p, bs_p = _mixer_call(xp, mod_p, zeros_s, zeros_bx, zeros_bs, lw, **mix_p)
        xs, s_s, bx_s, bs_s = _mixer_call(
            xs, mod_s, state_ssm[l].reshape(bs, hp, SSD_STATE), state_conv_ssd[l],
            state_conv_short[l], lw, **mix_s)

        xp = _ffn_call(xp, mod_p, 6, norm_ffn2[l], gu2, dn2, norm_final, final_norm=last, **ffn_p)
        xs = _ffn_call(xs, mod_s, 6, norm_ffn2[l], gu2, dn2, norm_final, final_norm=last, **ffn_s)

        for lst, v in zip(outs, (s_p, bx_p, bs_p, s_s, bx_s, bs_s)):
            lst.append(v)

    shp = (SSD_HEADS, SSD_HEAD_DIM, SSD_STATE)
    return (xp, xs,
            jnp.stack(outs[0]).reshape((depth, bp) + shp), jnp.stack(outs[1]), jnp.stack(outs[2]),
            jnp.stack(outs[3]).reshape((depth, bs) + shp), jnp.stack(outs[4]), jnp.stack(outs[5]))
```

```python
import functools

import jax
import jax.numpy as jnp
from jax import lax
from jax.experimental import pallas as pl
from jax.experimental.pallas import tpu as pltpu

F32 = jnp.float32
BF16 = jnp.bfloat16

D_MODEL = 1024
SSD_HEADS = 16
SSD_HEAD_DIM = 64
SSD_GROUPS = 4
SSD_STATE = 128
HEADS_PER_GROUP = SSD_HEADS // SSD_GROUPS
GROUP_WIDTH = HEADS_PER_GROUP * SSD_HEAD_DIM
SSD_INNER = SSD_HEADS * SSD_HEAD_DIM
D_XBC = SSD_INNER + 2 * SSD_GROUPS * SSD_STATE
SSD_CONV = 4
SC_CONV = 3
D_FF = 2816
N_MOD = 9
EPS = 1e-6

LANES = 128
SUBLANES = 8
HALO = SUBLANES
DT_PAD = LANES

C_Z = 0
C_XBC = C_Z + SSD_INNER
C_SCB = C_XBC + D_XBC
C_SCC = C_SCB + D_MODEL
C_SCH = C_SCC + D_MODEL
C_GSSD = C_SCH + D_MODEL
C_GSC = C_GSSD + D_MODEL
C_DT = C_GSC + D_MODEL
D_PROJ_PAD = C_DT + DT_PAD

FF_CHUNK = 256
FFN_ROWS = 512
MIX_ROWS_PROMPT = 256
MIX_BATCH_SAMPLE = 8
SSD_CHUNK = 128
VMEM_LIMIT = 56 * 1024 * 1024


def _dot(a, b):
    return jnp.dot(a, b, preferred_element_type=F32)


def _dot_nt(a, b):
    return lax.dot_general(a, b, (((1,), (1,)), ((), ())), preferred_element_type=F32)


def _dot_tn(a, b):
    return lax.dot_general(a, b, (((0,), (0,)), ((), ())), preferred_element_type=F32)


def _dot_hi_mid_rhs(x, sel):
    hi = x.astype(BF16)
    mid = (x - hi.astype(F32)).astype(BF16)
    return _dot(hi, sel) + _dot(mid, sel)


def _dot_exact_lhs(sel, x):
    hi = x.astype(BF16)
    r1 = x - hi.astype(F32)
    mid = r1.astype(BF16)
    lo = (r1 - mid.astype(F32)).astype(BF16)
    return _dot(sel, hi) + _dot(sel, mid) + _dot(sel, lo)


def _silu(x):
    return x / (1.0 + jnp.exp(-x))


def _sigmoid(x):
    return 1.0 / (1.0 + jnp.exp(-x))


def _softplus(x):
    return jnp.maximum(x, 0.0) + jnp.log1p(jnp.exp(-jnp.abs(x)))


def _rmsnorm(x, g):
    r = lax.rsqrt(jnp.mean(x * x, axis=-1, keepdims=True) + EPS)
    return (x * r) * g


def _layer_spec(shape, layer):
    nd = len(shape) - 1
    return pl.BlockSpec((1,) + tuple(shape[1:]), lambda *_: (layer,) + (0,) * nd,
                        pipeline_mode=pl.Buffered(1))


def _mod_spec(nb, row0, layer, k):
    return pl.BlockSpec((1, nb, 1, D_MODEL), lambda b, i: (layer, row0 // nb + b, 0, k))


def _mod_body(c_ref, w_ref, b_ref, o_ref):
    sc = _silu(c_ref[...]).astype(BF16)
    o_ref[0] = _dot(sc, w_ref[0].astype(BF16)) + b_ref[0]


def _mod_call(c_all, w_ada, b_ada):
    depth, d, n = w_ada.shape
    m = c_all.shape[0]
    tn = D_MODEL
    return pl.pallas_call(
        _mod_body,
        grid=(depth, n // tn),
        in_specs=[
            pl.BlockSpec((m, d), lambda l, j: (0, 0)),
            pl.BlockSpec((1, d, tn), lambda l, j: (l, 0, j)),
            pl.BlockSpec((1, 1, tn), lambda l, j: (l, 0, j)),
        ],
        out_specs=pl.BlockSpec((1, m, tn), lambda l, j: (l, 0, j)),
        out_shape=jax.ShapeDtypeStruct((depth, m, n), F32),
        compiler_params=pltpu.CompilerParams(
            dimension_semantics=("arbitrary", "arbitrary"),
            vmem_limit_bytes=VMEM_LIMIT),
        name="adaln_mod",
    )(c_all, w_ada, b_ada.reshape(depth, 1, n))


def _ffn_body(x_ref, sh_ref, sc_ref, gt_ref, nw_ref, wgu_ref, wd_ref, nf_ref,
              o_ref, act_ref, *, final_norm):
    x = x_ref[...]
    nb, lb, d = x.shape
    h = _rmsnorm(x, nw_ref[0]) * (1.0 + sc_ref[0]) + sh_ref[0]
    hb = h.reshape(nb * lb, d).astype(BF16)
    for c in range(D_FF // FF_CHUNK):
        lo = c * FF_CHUNK
        g = _dot(hb, wgu_ref[0, :, lo:lo + FF_CHUNK])
        u = _dot(hb, wgu_ref[0, :, D_FF + lo:D_FF + lo + FF_CHUNK])
        act_ref[:, lo:lo + FF_CHUNK] = (_silu(g) * u).astype(BF16)
    y = _dot(act_ref[...], wd_ref[0])
    out = x + (0.5 * gt_ref[0]) * y.reshape(nb, lb, d)
    if final_norm:
        out = _rmsnorm(out, nf_ref[...])
    o_ref[...] = out


def _ffn_call(x, mod, row0, k_mod, layer, norm_w, w_gu, w_down, norm_final, *, nb, lb,
              final_norm):
    NB, LB, d = x.shape
    rows = nb * lb
    return pl.pallas_call(
        functools.partial(_ffn_body, final_norm=final_norm),
        grid=(NB // nb, LB // lb),
        in_specs=[
            pl.BlockSpec((nb, lb, d), lambda b, i: (b, i, 0)),
            _mod_spec(nb, row0, layer, k_mod),
            _mod_spec(nb, row0, layer, k_mod + 1),
            _mod_spec(nb, row0, layer, k_mod + 2),
            _layer_spec(norm_w.shape, layer),
            _layer_spec(w_gu.shape, layer),
            _layer_spec(w_down.shape, layer),
            pl.BlockSpec((1, d), lambda b, i: (0, 0)),
        ],
        out_specs=pl.BlockSpec((nb, lb, d), lambda b, i: (b, i, 0)),
        out_shape=jax.ShapeDtypeStruct(x.shape, F32),
        scratch_shapes=[pltpu.VMEM((rows, D_FF), BF16)],
        compiler_params=pltpu.CompilerParams(
            dimension_semantics=("arbitrary", "arbitrary"),
            vmem_limit_bytes=VMEM_LIMIT),
        name="ffn",
    )(x, mod, mod, mod, norm_w, w_gu, w_down, norm_final.reshape(1, d))


def _causal_conv(ext_ref, cur, w_ref, new_buf_ref, lb, k_w):
    ext_ref[:, HALO:HALO + lb, :] = cur
    first = HALO - (k_w - 1)
    acc = None
    for k in range(k_w):
        tap = ext_ref[:, first + k:first + k + lb, :] * w_ref[0, k:k + 1, :]
        acc = tap if acc is None else acc + tap
    new_buf_ref[0] = ext_ref[:, lb + first:lb + HALO, :]
    ext_ref[:, 0:HALO, :] = ext_ref[:, lb:lb + HALO, :]
    return acc


def _mixer_body(x_ref, sh_ref, sc_ref, gt_ref, sin_ref, bxin_ref, bsin_ref,
                nw_ref, win_ref, cw_ref, cb_ref, dtb_ref, alog_ref, dexp_ref, snorm_ref,
                wos_ref, scw_ref, wosc_ref, wo_ref, *rest, nb, lb, chunk, n_alias):
    rest = rest[n_alias:]
    o_ref, sout_ref, bxout_ref, bsout_ref, extx_ref, exts_ref, y_ref = rest
    rows = nb * lb
    seg = min(lb, chunk)
    nseg = chunk // seg
    seg_shift = seg.bit_length() - 1
    assert seg == 1 << seg_shift and rows % chunk == 0

    @pl.when(pl.program_id(1) == 0)
    def _():
        sout_ref[0] = sin_ref[0]
        extx_ref[:, HALO - (SSD_CONV - 1):HALO, :] = bxin_ref[0]
        exts_ref[:, HALO - (SC_CONV - 1):HALO, :] = bsin_ref[0]

    x = x_ref[...]
    h = _rmsnorm(x, nw_ref[0]) * (1.0 + sc_ref[0]) + sh_ref[0]
    hb = h.reshape(rows, D_MODEL).astype(BF16)

    def proj(lo, width):
        return _dot(hb, win_ref[0, :, lo:lo + width])

    xbc = proj(C_XBC, D_XBC).reshape(nb, lb, D_XBC)
    xbc = _causal_conv(extx_ref, xbc, cw_ref, bxout_ref, lb, SSD_CONV)
    xbc = _silu(xbc + cb_ref[0]).reshape(rows, D_XBC)
    xs_all = xbc[:, 0:SSD_INNER]

    dt_all = _softplus(proj(C_DT, DT_PAD) + dtb_ref[0])
    da_all = dt_all * (-jnp.exp(alog_ref[0]))

    ri = lax.broadcasted_iota(jnp.int32, (chunk, chunk), 0)
    ci = lax.broadcasted_iota(jnp.int32, (chunk, chunk), 1)
    same_seq = (ri >> seg_shift) == (ci >> seg_shift)
    causal = jnp.logical_and(same_seq, ci <= ri)
    causal_sel = causal.astype(F32).astype(BF16)
    seq_sel = same_seq.astype(F32).astype(BF16)
    eh = lax.broadcasted_iota(jnp.int32, (LANES, SSD_INNER), 0)
    ej = lax.broadcasted_iota(jnp.int32, (LANES, SSD_INNER), 1)
    head_sel = (eh == ej // SSD_HEAD_DIM).astype(F32).astype(BF16)
    lane_head = lax.broadcasted_iota(jnp.int32, (chunk, GROUP_WIDTH), 1) // SSD_HEAD_DIM

    for c in range(rows // chunk):
        r0 = c * chunk
        xs = xbc[r0:r0 + chunk, 0:SSD_INNER]
        bm = xbc[r0:r0 + chunk, SSD_INNER:SSD_INNER + SSD_GROUPS * SSD_STATE]
        cm = xbc[r0:r0 + chunk, SSD_INNER + SSD_GROUPS * SSD_STATE:D_XBC]
        dt = dt_all[r0:r0 + chunk]
        da = da_all[r0:r0 + chunk]
        acum = _dot_exact_lhs(causal_sel, da)
        if nseg == 1:
            atot = acum[chunk - 1:chunk, :]
        else:
            atot = _dot_exact_lhs(seq_sel, da)
        ex = _dot_hi_mid_rhs(
            jnp.concatenate([dt * jnp.exp(atot - acum), jnp.exp(acum)], axis=0), head_sel)
        xd = xs * ex[0:chunk]
        e_a = ex[chunk:2 * chunk]
        xs_b = xs.astype(BF16)
        acum_t = acum.T
        dt_t = dt.T

        for g in range(SSD_GROUPS):
            gl = g * GROUP_WIDTH
            bg = bm[:, g * SSD_STATE:(g + 1) * SSD_STATE]
            cg = cm[:, g * SSD_STATE:(g + 1) * SSD_STATE]
            cbm = _dot_nt(cg.astype(BF16), bg.astype(BF16))
            ms = []
            for r in range(HEADS_PER_GROUP):
                hh = g * HEADS_PER_GROUP + r
                sgm = acum[:, hh:hh + 1] - acum_t[hh:hh + 1, :]
                dec = jnp.exp(jnp.where(causal, sgm, -jnp.inf))
                ms.append((cbm * dec * dt_t[hh:hh + 1, :]).astype(BF16))
            mg = jnp.concatenate(ms, axis=1)
            xg = xs_b[:, gl:gl + GROUP_WIDTH]
            rhs = jnp.concatenate(
                [jnp.where(lane_head == r, xg, jnp.zeros_like(xg))
                 for r in range(HEADS_PER_GROUP)], axis=0)
            y_diag = _dot(mg, rhs)

            for j in range(nseg):
                q0 = j * seg
                b_loc = (r0 + q0) // lb
                s_old = sout_ref[0, b_loc, gl:gl + GROUP_WIDTH, :]
                y_off = _dot_nt(cg[q0:q0 + seg].astype(BF16), s_old.astype(BF16))
                y_ref[r0 + q0:r0 + q0 + seg, gl:gl + GROUP_WIDTH] = (
                    y_diag[q0:q0 + seg] + y_off * e_a[q0:q0 + seg, gl:gl + GROUP_WIDTH])
                s_add = _dot_tn(xd[q0:q0 + seg, gl:gl + GROUP_WIDTH].astype(BF16),
                                bg[q0:q0 + seg].astype(BF16))
                q_last = q0 + seg - 1 if nseg > 1 else 0
                for r in range(HEADS_PER_GROUP):
                    hh = g * HEADS_PER_GROUP + r
                    keep = jnp.exp(atot[q_last:q_last + 1, hh:hh + 1])
                    p0 = r * SSD_HEAD_DIM
                    sout_ref[0, b_loc, gl + p0:gl + p0 + SSD_HEAD_DIM, :] = (
                        keep * s_old[p0:p0 + SSD_HEAD_DIM] + s_add[p0:p0 + SSD_HEAD_DIM])

    y = y_ref[...] + xs_all * dexp_ref[0]
    y = y * _silu(proj(C_Z, SSD_INNER))
    parts = []
    for g in range(SSD_GROUPS):
        yg = y[:, g * GROUP_WIDTH:(g + 1) * GROUP_WIDTH]
        parts.append(yg * lax.rsqrt(jnp.mean(yg * yg, axis=-1, keepdims=True) + EPS))
    yn = jnp.concatenate(parts, axis=1) * snorm_ref[0]
    y_ssd = _dot(yn.astype(BF16), wos_ref[0])

    v = (proj(C_SCC, D_MODEL) * proj(C_SCH, D_MODEL)).reshape(nb, lb, D_MODEL)
    u = _causal_conv(exts_ref, v, scw_ref, bsout_ref, lb, SC_CONV).reshape(rows, D_MODEL)
    y_sc = _dot((proj(C_SCB, D_MODEL) * u).astype(BF16), wosc_ref[0])

    merged = (_sigmoid(proj(C_GSSD, D_MODEL)) * y_ssd
              + _sigmoid(proj(C_GSC, D_MODEL)) * y_sc)
    o_ref[...] = x + gt_ref[0] * _dot(merged.astype(BF16), wo_ref[0]).reshape(nb, lb, D_MODEL)


def _mixer_call(x, mod, row0, layer, state_layer, s_in, bx_in, bs_in, lw, prev, *, nb, lb,
                chunk):
    NB, LB, d = x.shape
    rows = nb * lb
    depth = lw["w_in"].shape[0]

    def per_seq(arr, lead):
        return pl.BlockSpec((1, nb) + arr.shape[2:], lambda b, i: (lead, b, 0, 0))

    consts = [lw["norm_mix"], lw["w_in"], lw["ssd_conv_w"], lw["ssd_conv_b"], lw["dt_bias"],
              lw["a_log"], lw["d_exp"], lw["ssd_norm"], lw["w_out_ssd"], lw["sc_conv_w"],
              lw["w_out_sc"], lw["w_o"]]
    state_shapes = [(depth,) + a.shape[1:] for a in (s_in, bx_in, bs_in)]
    n_in = 7 + len(consts)
    prev = list(prev) if prev is not None else []
    return pl.pallas_call(
        functools.partial(_mixer_body, nb=nb, lb=lb, chunk=chunk, n_alias=len(prev)),
        grid=(NB // nb, LB // lb),
        in_specs=[
            pl.BlockSpec((nb, lb, d), lambda b, i: (b, i, 0)),
            _mod_spec(nb, row0, layer, 3), _mod_spec(nb, row0, layer, 4),
            _mod_spec(nb, row0, layer, 5),
            per_seq(s_in, state_layer), per_seq(bx_in, state_layer), per_seq(bs_in, state_layer),
        ] + [_layer_spec(w.shape, layer) for w in consts]
          + [pl.BlockSpec(memory_space=pl.ANY) for _ in prev],
        out_specs=[
            pl.BlockSpec((nb, lb, d), lambda b, i: (b, i, 0)),
            per_seq(s_in, layer), per_seq(bx_in, layer), per_seq(bs_in, layer),
        ],
        out_shape=[jax.ShapeDtypeStruct(x.shape, F32)]
                  + [jax.ShapeDtypeStruct(s, F32) for s in state_shapes],
        input_output_aliases={n_in + k: 1 + k for k in range(len(prev))},
        scratch_shapes=[
            pltpu.VMEM((nb, HALO + lb, D_XBC), F32),
            pltpu.VMEM((nb, HALO + lb, D_MODEL), F32),
            pltpu.VMEM((rows, SSD_INNER), F32),
        ],
        compiler_params=pltpu.CompilerParams(
            dimension_semantics=("arbitrary", "arbitrary"),
            vmem_limit_bytes=VMEM_LIMIT),
        name="mixer",
    )(x, mod, mod, mod, s_in, bx_in, bs_in, *consts, *prev)


def _mixer_weights(w_in, norm_mix, ssd_conv_w, ssd_conv_b, ssd_dt_bias, ssd_a_log, ssd_d,
                   ssd_norm, w_out_ssd, sc_conv_w, w_out_sc, w_o):
    depth, d, _ = w_in.shape
    dt_lo = SSD_INNER + D_XBC
    dt_hi = dt_lo + SSD_HEADS
    w_re = jnp.concatenate(
        [w_in[:, :, :dt_lo].astype(BF16), w_in[:, :, dt_hi:].astype(BF16),
         w_in[:, :, dt_lo:dt_hi].astype(BF16),
         jnp.zeros((depth, d, DT_PAD - SSD_HEADS), BF16)], axis=2)
    pad = lambda v: jnp.pad(v, ((0, 0), (0, DT_PAD - SSD_HEADS))).reshape(depth, 1, DT_PAD)
    return {
        "norm_mix": norm_mix.reshape(depth, 1, d),
        "w_in": w_re,
        "ssd_conv_w": ssd_conv_w,
        "ssd_conv_b": ssd_conv_b.reshape(depth, 1, D_XBC),
        "dt_bias": pad(ssd_dt_bias),
        "a_log": pad(ssd_a_log),
        "d_exp": jnp.repeat(ssd_d, SSD_HEAD_DIM, axis=1).reshape(depth, 1, SSD_INNER),
        "ssd_norm": ssd_norm.reshape(depth, 1, SSD_INNER),
        "w_out_ssd": w_out_ssd.astype(BF16),
        "sc_conv_w": sc_conv_w,
        "w_out_sc": w_out_sc.astype(BF16),
        "w_o": w_o.astype(BF16),
    }


def kernel(x_prompt, x_sample, c_prompt, c_sample, state_ssm, state_conv_ssd, state_conv_short, w_ada, b_ada, norm_ffn1, norm_mix, norm_ffn2, ffn1_w_gu, ffn1_w_down, ffn2_w_gu, ffn2_w_down, w_in, ssd_conv_w, ssd_conv_b, ssd_dt_bias, ssd_a_log, ssd_d, ssd_norm, w_out_ssd, sc_conv_w, w_out_sc, w_o, norm_final):
    depth = w_ada.shape[0]
    bp, lp, d = x_prompt.shape
    bs, ls, _ = x_sample.shape
    hp = SSD_HEADS * SSD_HEAD_DIM

    mod = _mod_call(jnp.concatenate([c_sample, c_prompt], axis=0), w_ada, b_ada)
    mod = mod.reshape(depth, bs + bp, 1, N_MOD * d)
    row_s, row_p = 0, bs

    zeros_s = jnp.zeros((1, bp, hp, SSD_STATE), F32)
    zeros_bx = jnp.zeros((1, bp, SSD_CONV - 1, D_XBC), F32)
    zeros_bs = jnp.zeros((1, bp, SC_CONV - 1, D_MODEL), F32)
    state_s = state_ssm.reshape(depth, bs, hp, SSD_STATE)

    ffn_p = dict(nb=1, lb=min(FFN_ROWS, lp))
    ffn_s = dict(nb=min(FFN_ROWS // ls, bs), lb=ls)
    mix_p = dict(nb=1, lb=min(MIX_ROWS_PROMPT, lp), chunk=min(SSD_CHUNK, lp))
    mix_s = dict(nb=min(MIX_BATCH_SAMPLE, bs), lb=ls, chunk=min(MIX_BATCH_SAMPLE, bs) * ls)

    n1 = norm_ffn1.reshape(depth, 1, d)
    n2 = norm_ffn2.reshape(depth, 1, d)
    gu1, dn1 = ffn1_w_gu.astype(BF16), ffn1_w_down.astype(BF16)
    gu2, dn2 = ffn2_w_gu.astype(BF16), ffn2_w_down.astype(BF16)
    lw = _mixer_weights(w_in, norm_mix, ssd_conv_w, ssd_conv_b, ssd_dt_bias, ssd_a_log, ssd_d,
                        ssd_norm, w_out_ssd, sc_conv_w, w_out_sc, w_o)

    xp, xs = x_prompt, x_sample
    st_p = st_s = None
    for l in range(depth):
        last = l == depth - 1
        xp = _ffn_call(xp, mod, row_p, 0, l, n1, gu1, dn1, norm_final, final_norm=False, **ffn_p)
        xs = _ffn_call(xs, mod, row_s, 0, l, n1, gu1, dn1, norm_final, final_norm=False, **ffn_s)

        xp, *st_p = _mixer_call(xp, mod, row_p, l, 0, zeros_s, zeros_bx, zeros_bs, lw, st_p,
                                **mix_p)
        xs, *st_s = _mixer_call(xs, mod, row_s, l, l, state_s, state_conv_ssd, state_conv_short,
                                lw, st_s, **mix_s)

        xp = _ffn_call(xp, mod, row_p, 6, l, n2, gu2, dn2, norm_final, final_norm=last, **ffn_p)
        xs = _ffn_call(xs, mod, row_s, 6, l, n2, gu2, dn2, norm_final, final_norm=last, **ffn_s)

    shp = (SSD_HEADS, SSD_HEAD_DIM, SSD_STATE)
    return (xp, xs,
            st_p[0].reshape((depth, bp) + shp), st_p[1], st_p[2],
            st_s[0].reshape((depth, bs) + shp), st_s[1], st_s[2])
```

```python
import functools

import jax
import jax.numpy as jnp
from jax import lax
from jax.experimental import pallas as pl
from jax.experimental.pallas import tpu as pltpu

F32 = jnp.float32
BF16 = jnp.bfloat16

D_MODEL = 1024
SSD_HEADS = 16
SSD_HEAD_DIM = 64
SSD_GROUPS = 4
SSD_STATE = 128
HEADS_PER_GROUP = SSD_HEADS // SSD_GROUPS
GROUP_WIDTH = HEADS_PER_GROUP * SSD_HEAD_DIM
SSD_INNER = SSD_HEADS * SSD_HEAD_DIM
D_XBC = SSD_INNER + 2 * SSD_GROUPS * SSD_STATE
SSD_CONV = 4
SC_CONV = 3
D_FF = 2816
N_MOD = 9
EPS = 1e-6

LANES = 128
SUBLANES = 8
HALO = SUBLANES
DT_PAD = LANES

C_Z = 0
C_XBC = C_Z + SSD_INNER
C_SCB = C_XBC + D_XBC
C_SCC = C_SCB + D_MODEL
C_SCH = C_SCC + D_MODEL
C_GSSD = C_SCH + D_MODEL
C_GSC = C_GSSD + D_MODEL
C_DT = C_GSC + D_MODEL
D_PROJ_PAD = C_DT + DT_PAD

FF_CHUNK = 256
FFN_ROWS = 512
MIX_ROWS_PROMPT = 256
MIX_BATCH_SAMPLE = 8
SSD_CHUNK = 128
VMEM_LIMIT = 56 * 1024 * 1024


def _dot(a, b):
    return jnp.dot(a, b, preferred_element_type=F32)


def _dot_nt(a, b):
    return lax.dot_general(a, b, (((1,), (1,)), ((), ())), preferred_element_type=F32)


def _dot_tn(a, b):
    return lax.dot_general(a, b, (((0,), (0,)), ((), ())), preferred_element_type=F32)


def _dot_hi_mid_rhs(x, sel):
    hi = x.astype(BF16)
    mid = (x - hi.astype(F32)).astype(BF16)
    return _dot(hi, sel) + _dot(mid, sel)


def _dot_exact_lhs(sel, x):
    hi = x.astype(BF16)
    r1 = x - hi.astype(F32)
    mid = r1.astype(BF16)
    lo = (r1 - mid.astype(F32)).astype(BF16)
    return _dot(sel, hi) + _dot(sel, mid) + _dot(sel, lo)


def _silu(x):
    return x / (1.0 + jnp.exp(-x))


def _sigmoid(x):
    return 1.0 / (1.0 + jnp.exp(-x))


def _softplus(x):
    return jnp.maximum(x, 0.0) + jnp.log1p(jnp.exp(-jnp.abs(x)))


def _rmsnorm(x, g):
    r = lax.rsqrt(jnp.mean(x * x, axis=-1, keepdims=True) + EPS)
    return (x * r) * g


def _layer_spec(shape, layer):
    nd = len(shape) - 1
    return pl.BlockSpec((1,) + tuple(shape[1:]), lambda *_: (layer,) + (0,) * nd,
                        pipeline_mode=pl.Buffered(1))


def _mod_spec(nb, row0, layer, k):
    return pl.BlockSpec((1, nb, 1, D_MODEL), lambda b, i: (layer, row0 // nb + b, 0, k))


def _mod_body(c_ref, w_ref, b_ref, o_ref):
    sc = _silu(c_ref[...]).astype(BF16)
    o_ref[0] = _dot(sc, w_ref[0].astype(BF16)) + b_ref[0]


def _mod_call(c_all, w_ada, b_ada):
    depth, d, n = w_ada.shape
    m = c_all.shape[0]
    tn = D_MODEL
    return pl.pallas_call(
        _mod_body,
        grid=(depth, n // tn),
        in_specs=[
            pl.BlockSpec((m, d), lambda l, j: (0, 0)),
            pl.BlockSpec((1, d, tn), lambda l, j: (l, 0, j)),
            pl.BlockSpec((1, 1, tn), lambda l, j: (l, 0, j)),
        ],
        out_specs=pl.BlockSpec((1, m, tn), lambda l, j: (l, 0, j)),
        out_shape=jax.ShapeDtypeStruct((depth, m, n), F32),
        compiler_params=pltpu.CompilerParams(
            dimension_semantics=("arbitrary", "arbitrary"),
            vmem_limit_bytes=VMEM_LIMIT),
        name="adaln_mod",
    )(c_all, w_ada, b_ada.reshape(depth, 1, n))


def _ffn_body(x_ref, sh_ref, sc_ref, gt_ref, nw_ref, wgu_ref, wd_ref, nf_ref,
              o_ref, act_ref, *, final_norm):
    x = x_ref[...]
    nb, lb, d = x.shape
    h = _rmsnorm(x, nw_ref[0]) * (1.0 + sc_ref[0]) + sh_ref[0]
    hb = h.reshape(nb * lb, d).astype(BF16)
    for c in range(D_FF // FF_CHUNK):
        lo = c * FF_CHUNK
        g = _dot(hb, wgu_ref[0, :, lo:lo + FF_CHUNK])
        u = _dot(hb, wgu_ref[0, :, D_FF + lo:D_FF + lo + FF_CHUNK])
        act_ref[:, lo:lo + FF_CHUNK] = (_silu(g) * u).astype(BF16)
    y = _dot(act_ref[...], wd_ref[0])
    out = x + (0.5 * gt_ref[0]) * y.reshape(nb, lb, d)
    if final_norm:
        out = _rmsnorm(out, nf_ref[...])
    o_ref[...] = out


def _ffn_call(x, mod, row0, k_mod, layer, norm_w, w_gu, w_down, norm_final, *, nb, lb,
              final_norm):
    NB, LB, d = x.shape
    rows = nb * lb
    return pl.pallas_call(
        functools.partial(_ffn_body, final_norm=final_norm),
        grid=(NB // nb, LB // lb),
        in_specs=[
            pl.BlockSpec((nb, lb, d), lambda b, i: (b, i, 0)),
            _mod_spec(nb, row0, layer, k_mod),
            _mod_spec(nb, row0, layer, k_mod + 1),
            _mod_spec(nb, row0, layer, k_mod + 2),
            _layer_spec(norm_w.shape, layer),
            _layer_spec(w_gu.shape, layer),
            _layer_spec(w_down.shape, layer),
            pl.BlockSpec((1, d), lambda b, i: (0, 0)),
        ],
        out_specs=pl.BlockSpec((nb, lb, d), lambda b, i: (b, i, 0)),
        out_shape=jax.ShapeDtypeStruct(x.shape, F32),
        scratch_shapes=[pltpu.VMEM((rows, D_FF), BF16)],
        compiler_params=pltpu.CompilerParams(
            dimension_semantics=("arbitrary", "arbitrary"),
            vmem_limit_bytes=VMEM_LIMIT),
        name="ffn",
    )(x, mod, mod, mod, norm_w, w_gu, w_down, norm_final.reshape(1, d))


def _causal_conv(ext_ref, cur, w_ref, new_buf_ref, lb, k_w, c0):
    cols = slice(c0, c0 + cur.shape[-1])
    ext_ref[:, HALO:HALO + lb, cols] = cur
    first = HALO - (k_w - 1)
    acc = None
    for k in range(k_w):
        tap = ext_ref[:, first + k:first + k + lb, cols] * w_ref[0, k:k + 1, cols]
        acc = tap if acc is None else acc + tap
    new_buf_ref[0, :, :, cols] = ext_ref[:, lb + first:lb + HALO, cols]
    ext_ref[:, 0:HALO, cols] = ext_ref[:, lb:lb + HALO, cols]
    return acc


def _mixer_body(x_ref, sh_ref, sc_ref, gt_ref, sin_ref, bxin_ref, bsin_ref,
                nw_ref, win_ref, cw_ref, cb_ref, dtb_ref, alog_ref, dexp_ref, snorm_ref,
                wos_ref, scw_ref, wosc_ref, wo_ref, *rest, nb, lb, chunk, n_alias):
    rest = rest[n_alias:]
    o_ref, sout_ref, bxout_ref, bsout_ref, extx_ref, exts_ref, xbc_ref, y_ref = rest
    rows = nb * lb
    seg = min(lb, chunk)
    nseg = chunk // seg
    seg_shift = seg.bit_length() - 1
    assert seg == 1 << seg_shift and rows % chunk == 0

    @pl.when(pl.program_id(1) == 0)
    def _():
        sout_ref[0] = sin_ref[0]
        extx_ref[:, HALO - (SSD_CONV - 1):HALO, :] = bxin_ref[0]
        exts_ref[:, HALO - (SC_CONV - 1):HALO, :] = bsin_ref[0]

    x = x_ref[...]
    h = _rmsnorm(x, nw_ref[0]) * (1.0 + sc_ref[0]) + sh_ref[0]
    hb = h.reshape(rows, D_MODEL).astype(BF16)

    def proj(lo, width):
        return _dot(hb, win_ref[0, :, lo:lo + width])

    half = D_MODEL // 2
    side = {}

    def side_scc(hf):
        side["scc", hf] = proj(C_SCC + hf * half, half)

    def side_v(hf):
        side["v", hf] = side.pop(("scc", hf)) * proj(C_SCH + hf * half, half)

    def side_conv(_):
        v = jnp.concatenate([side.pop(("v", 0)), side.pop(("v", 1))], axis=1)
        side["u"] = _causal_conv(exts_ref, v.reshape(nb, lb, D_MODEL), scw_ref, bsout_ref, lb,
                                 SC_CONV, 0).reshape(rows, D_MODEL)

    def side_su(hf):
        lo = hf * half
        side["su", hf] = (proj(C_SCB + lo, half) * side["u"][:, lo:lo + half]).astype(BF16)

    def side_ysc(hf):
        if hf == 0:
            side["su"] = jnp.concatenate([side.pop(("su", 0)), side.pop(("su", 1))], axis=1)
        side["ysc", hf] = _dot(side["su"], wosc_ref[0, :, hf * half:(hf + 1) * half])

    def side_gsc(hf):
        side["gsc", hf] = _sigmoid(proj(C_GSC + hf * half, half)) * side.pop(("ysc", hf))

    def side_gssd(hf):
        side["gssd", hf] = _sigmoid(proj(C_GSSD + hf * half, half))

    def side_z(hf):
        side["z", hf] = _silu(proj(C_Z + hf * half, half))

    queue = [functools.partial(f, hf) for f, hf in (
        (side_scc, 0), (side_v, 0), (side_scc, 1), (side_v, 1), (side_conv, 0),
        (side_su, 0), (side_su, 1), (side_ysc, 0), (side_ysc, 1), (side_gsc, 0), (side_gsc, 1),
        (side_gssd, 0), (side_gssd, 1), (side_z, 0), (side_z, 1))]

    def side_step():
        if queue:
            queue.pop(0)()

    for c0 in range(0, D_XBC, GROUP_WIDTH):
        piece = proj(C_XBC + c0, GROUP_WIDTH).reshape(nb, lb, GROUP_WIDTH)
        piece = _causal_conv(extx_ref, piece, cw_ref, bxout_ref, lb, SSD_CONV, c0)
        piece = _silu(piece + cb_ref[0, :, c0:c0 + GROUP_WIDTH])
        xbc_ref[:, c0:c0 + GROUP_WIDTH] = piece.reshape(rows, GROUP_WIDTH)
        side_step()

    dt_all = _softplus(proj(C_DT, DT_PAD) + dtb_ref[0])
    da_all = dt_all * (-jnp.exp(alog_ref[0]))

    ri = lax.broadcasted_iota(jnp.int32, (chunk, chunk), 0)
    ci = lax.broadcasted_iota(jnp.int32, (chunk, chunk), 1)
    same_seq = (ri >> seg_shift) == (ci >> seg_shift)
    causal = jnp.logical_and(same_seq, ci <= ri)
    causal_sel = causal.astype(F32).astype(BF16)
    seq_sel = same_seq.astype(F32).astype(BF16)
    eh = lax.broadcasted_iota(jnp.int32, (LANES, SSD_INNER), 0)
    ej = lax.broadcasted_iota(jnp.int32, (LANES, SSD_INNER), 1)
    head_sel = (eh == ej // SSD_HEAD_DIM).astype(F32).astype(BF16)
    lane_head = lax.broadcasted_iota(jnp.int32, (chunk, GROUP_WIDTH), 1) // SSD_HEAD_DIM
    bc0 = SSD_INNER
    cc0 = SSD_INNER + SSD_GROUPS * SSD_STATE

    for c in range(rows // chunk):
        r0 = c * chunk
        xs = xbc_ref[r0:r0 + chunk, 0:SSD_INNER]
        dt = dt_all[r0:r0 + chunk]
        da = da_all[r0:r0 + chunk]
        acum = _dot_exact_lhs(causal_sel, da)
        if nseg == 1:
            atot = acum[chunk - 1:chunk, :]
        else:
            atot = _dot_exact_lhs(seq_sel, da)
        ex = _dot_hi_mid_rhs(
            jnp.concatenate([dt * jnp.exp(atot - acum), jnp.exp(acum)], axis=0), head_sel)
        xd = xs * ex[0:chunk]
        e_a = ex[chunk:2 * chunk]
        xs_b = xs.astype(BF16)
        acum_t = acum.T
        dt_t = dt.T
        side_step()

        for g in range(SSD_GROUPS):
            gl = g * GROUP_WIDTH
            bg = xbc_ref[r0:r0 + chunk, bc0 + g * SSD_STATE:bc0 + (g + 1) * SSD_STATE]
            cg = xbc_ref[r0:r0 + chunk, cc0 + g * SSD_STATE:cc0 + (g + 1) * SSD_STATE]
            cbm = _dot_nt(cg.astype(BF16), bg.astype(BF16))
            ms = []
            for r in range(HEADS_PER_GROUP):
                hh = g * HEADS_PER_GROUP + r
                sgm = acum[:, hh:hh + 1] - acum_t[hh:hh + 1, :]
                dec = jnp.exp(jnp.where(causal, sgm, -jnp.inf))
                ms.append((cbm * dec * dt_t[hh:hh + 1, :]).astype(BF16))
            mg = jnp.concatenate(ms, axis=1)
            xg = xs_b[:, gl:gl + GROUP_WIDTH]
            rhs = jnp.concatenate(
                [jnp.where(lane_head == r, xg, jnp.zeros_like(xg))
                 for r in range(HEADS_PER_GROUP)], axis=0)
            y_diag = _dot(mg, rhs)

            for j in range(nseg):
                q0 = j * seg
                b_loc = (r0 + q0) // lb
                s_old = sout_ref[0, b_loc, gl:gl + GROUP_WIDTH, :]
                y_off = _dot_nt(cg[q0:q0 + seg].astype(BF16), s_old.astype(BF16))
                y_ref[r0 + q0:r0 + q0 + seg, gl:gl + GROUP_WIDTH] = (
                    y_diag[q0:q0 + seg] + y_off * e_a[q0:q0 + seg, gl:gl + GROUP_WIDTH])
                s_add = _dot_tn(xd[q0:q0 + seg, gl:gl + GROUP_WIDTH].astype(BF16),
                                bg[q0:q0 + seg].astype(BF16))
                q_last = q0 + seg - 1 if nseg > 1 else 0
                for r in range(HEADS_PER_GROUP):
                    hh = g * HEADS_PER_GROUP + r
                    keep = jnp.exp(atot[q_last:q_last + 1, hh:hh + 1])
                    p0 = r * SSD_HEAD_DIM
                    sout_ref[0, b_loc, gl + p0:gl + p0 + SSD_HEAD_DIM, :] = (
                        keep * s_old[p0:p0 + SSD_HEAD_DIM] + s_add[p0:p0 + SSD_HEAD_DIM])
            side_step()

    while queue:
        side_step()

    z_gate = jnp.concatenate([side["z", 0], side["z", 1]], axis=1)
    y = (y_ref[...] + xbc_ref[:, 0:SSD_INNER] * dexp_ref[0]) * z_gate
    parts = []
    for g in range(SSD_GROUPS):
        yg = y[:, g * GROUP_WIDTH:(g + 1) * GROUP_WIDTH]
        parts.append(yg * lax.rsqrt(jnp.mean(yg * yg, axis=-1, keepdims=True) + EPS))
    yn = jnp.concatenate(parts, axis=1) * snorm_ref[0]
    y_ssd = _dot(yn.astype(BF16), wos_ref[0])

    merged = (jnp.concatenate([side["gssd", 0], side["gssd", 1]], axis=1) * y_ssd
              + jnp.concatenate([side["gsc", 0], side["gsc", 1]], axis=1))
    o_ref[...] = x + gt_ref[0] * _dot(merged.astype(BF16), wo_ref[0]).reshape(nb, lb, D_MODEL)


def _mixer_call(x, mod, row0, layer, state_layer, s_in, bx_in, bs_in, lw, prev, *, nb, lb,
                chunk):
    NB, LB, d = x.shape
    rows = nb * lb
    depth = lw["w_in"].shape[0]

    def per_seq(arr, lead):
        return pl.BlockSpec((1, nb) + arr.shape[2:], lambda b, i: (lead, b, 0, 0))

    consts = [lw["norm_mix"], lw["w_in"], lw["ssd_conv_w"], lw["ssd_conv_b"], lw["dt_bias"],
              lw["a_log"], lw["d_exp"], lw["ssd_norm"], lw["w_out_ssd"], lw["sc_conv_w"],
              lw["w_out_sc"], lw["w_o"]]
    state_shapes = [(depth,) + a.shape[1:] for a in (s_in, bx_in, bs_in)]
    n_in = 7 + len(consts)
    prev = list(prev) if prev is not None else []
    return pl.pallas_call(
        functools.partial(_mixer_body, nb=nb, lb=lb, chunk=chunk, n_alias=len(prev)),
        grid=(NB // nb, LB // lb),
        in_specs=[
            pl.BlockSpec((nb, lb, d), lambda b, i: (b, i, 0)),
            _mod_spec(nb, row0, layer, 3), _mod_spec(nb, row0, layer, 4),
            _mod_spec(nb, row0, layer, 5),
            per_seq(s_in, state_layer), per_seq(bx_in, state_layer), per_seq(bs_in, state_layer),
        ] + [_layer_spec(w.shape, layer) for w in consts]
          + [pl.BlockSpec(memory_space=pl.ANY) for _ in prev],
        out_specs=[
            pl.BlockSpec((nb, lb, d), lambda b, i: (b, i, 0)),
            per_seq(s_in, layer), per_seq(bx_in, layer), per_seq(bs_in, layer),
        ],
        out_shape=[jax.ShapeDtypeStruct(x.shape, F32)]
                  + [jax.ShapeDtypeStruct(s, F32) for s in state_shapes],
        input_output_aliases={n_in + k: 1 + k for k in range(len(prev))},
        scratch_shapes=[
            pltpu.VMEM((nb, HALO + lb, D_XBC), F32),
            pltpu.VMEM((nb, HALO + lb, D_MODEL), F32),
            pltpu.VMEM((rows, D_XBC), F32),
            pltpu.VMEM((rows, SSD_INNER), F32),
        ],
        compiler_params=pltpu.CompilerParams(
            dimension_semantics=("arbitrary", "arbitrary"),
            vmem_limit_bytes=VMEM_LIMIT),
        name="mixer",
    )(x, mod, mod, mod, s_in, bx_in, bs_in, *consts, *prev)


def _mixer_weights(w_in, norm_mix, ssd_conv_w, ssd_conv_b, ssd_dt_bias, ssd_a_log, ssd_d,
                   ssd_norm, w_out_ssd, sc_conv_w, w_out_sc, w_o):
    depth, d, _ = w_in.shape
    dt_lo = SSD_INNER + D_XBC
    dt_hi = dt_lo + SSD_HEADS
    w_b = w_in.astype(BF16)
    w_re = jnp.concatenate(
        [w_b[:, :, :dt_lo], w_b[:, :, dt_hi:], w_b[:, :, dt_lo:dt_hi],
         jnp.zeros((depth, d, DT_PAD - SSD_HEADS), BF16)], axis=2)
    pad = lambda v: jnp.pad(v, ((0, 0), (0, DT_PAD - SSD_HEADS))).reshape(depth, 1, DT_PAD)
    return {
        "norm_mix": norm_mix.reshape(depth, 1, d),
        "w_in": w_re,
        "ssd_conv_w": ssd_conv_w,
        "ssd_conv_b": ssd_conv_b.reshape(depth, 1, D_XBC),
        "dt_bias": pad(ssd_dt_bias),
        "a_log": pad(ssd_a_log),
        "d_exp": jnp.repeat(ssd_d, SSD_HEAD_DIM, axis=1).reshape(depth, 1, SSD_INNER),
        "ssd_norm": ssd_norm.reshape(depth, 1, SSD_INNER),
        "w_out_ssd": w_out_ssd.astype(BF16),
        "sc_conv_w": sc_conv_w,
        "w_out_sc": w_out_sc.astype(BF16),
        "w_o": w_o.astype(BF16),
    }


def kernel(x_prompt, x_sample, c_prompt, c_sample, state_ssm, state_conv_ssd, state_conv_short, w_ada, b_ada, norm_ffn1, norm_mix, norm_ffn2, ffn1_w_gu, ffn1_w_down, ffn2_w_gu, ffn2_w_down, w_in, ssd_conv_w, ssd_conv_b, ssd_dt_bias, ssd_a_log, ssd_d, ssd_norm, w_out_ssd, sc_conv_w, w_out_sc, w_o, norm_final):
    depth = w_ada.shape[0]
    bp, lp, d = x_prompt.shape
    bs, ls, _ = x_sample.shape
    hp = SSD_HEADS * SSD_HEAD_DIM

    mod = _mod_call(jnp.concatenate([c_sample, c_prompt], axis=0), w_ada, b_ada)
    mod = mod.reshape(depth, bs + bp, 1, N_MOD * d)
    row_s, row_p = 0, bs

    zeros_s = jnp.zeros((1, bp, hp, SSD_STATE), F32)
    zeros_bx = jnp.zeros((1, bp, SSD_CONV - 1, D_XBC), F32)
    zeros_bs = jnp.zeros((1, bp, SC_CONV - 1, D_MODEL), F32)
    state_s = state_ssm.reshape(depth, bs, hp, SSD_STATE)

    ffn_p = dict(nb=1, lb=min(FFN_ROWS, lp))
    ffn_s = dict(nb=min(FFN_ROWS // ls, bs), lb=ls)
    mix_p = dict(nb=1, lb=min(MIX_ROWS_PROMPT, lp), chunk=min(SSD_CHUNK, lp))
    mix_s = dict(nb=min(MIX_BATCH_SAMPLE, bs), lb=ls, chunk=min(MIX_BATCH_SAMPLE, bs) * ls)

    n1 = norm_ffn1.reshape(depth, 1, d)
    n2 = norm_ffn2.reshape(depth, 1, d)
    gu1, dn1 = ffn1_w_gu.astype(BF16), ffn1_w_down.astype(BF16)
    gu2, dn2 = ffn2_w_gu.astype(BF16), ffn2_w_down.astype(BF16)
    lw = _mixer_weights(w_in, norm_mix, ssd_conv_w, ssd_conv_b, ssd_dt_bias, ssd_a_log, ssd_d,
                        ssd_norm, w_out_ssd, sc_conv_w, w_out_sc, w_o)

    xp, xs = x_prompt, x_sample
    st_p = st_s = None
    for l in range(depth):
        last = l == depth - 1
        xp = _ffn_call(xp, mod, row_p, 0, l, n1, gu1, dn1, norm_final, final_norm=False, **ffn_p)
        xs = _ffn_call(xs, mod, row_s, 0, l, n1, gu1, dn1, norm_final, final_norm=False, **ffn_s)

        xp, *st_p = _mixer_call(xp, mod, row_p, l, 0, zeros_s, zeros_bx, zeros_bs, lw, st_p,
                                **mix_p)
        xs, *st_s = _mixer_call(xs, mod, row_s, l, l, state_s, state_conv_ssd, state_conv_short,
                                lw, st_s, **mix_s)

        xp = _ffn_call(xp, mod, row_p, 6, l, n2, gu2, dn2, norm_final, final_norm=last, **ffn_p)
        xs = _ffn_call(xs, mod, row_s, 6, l, n2, gu2, dn2, norm_final, final_norm=last, **ffn_s)

    shp = (SSD_HEADS, SSD_HEAD_DIM, SSD_STATE)
    return (xp, xs,
            st_p[0].reshape((depth, bp) + shp), st_p[1], st_p[2],
            st_s[0].reshape((depth, bs) + shp), st_s[1], st_s[2])
```

```python
import functools

import jax
import jax.numpy as jnp
from jax import lax
from jax.experimental import pallas as pl
from jax.experimental.pallas import tpu as pltpu

F32 = jnp.float32
BF16 = jnp.bfloat16

D_MODEL = 1024
SSD_HEADS = 16
SSD_HEAD_DIM = 64
SSD_GROUPS = 4
SSD_STATE = 128
HEADS_PER_GROUP = SSD_HEADS // SSD_GROUPS
GROUP_WIDTH = HEADS_PER_GROUP * SSD_HEAD_DIM
SSD_INNER = SSD_HEADS * SSD_HEAD_DIM
D_XBC = SSD_INNER + 2 * SSD_GROUPS * SSD_STATE
SSD_CONV = 4
SC_CONV = 3
D_FF = 2816
N_MOD = 9
EPS = 1e-6

LANES = 128
SUBLANES = 8
HALO = SUBLANES
DT_PAD = LANES

C_Z = 0
C_XBC = C_Z + SSD_INNER
C_SCB = C_XBC + D_XBC
C_SCC = C_SCB + D_MODEL
C_SCH = C_SCC + D_MODEL
C_GSSD = C_SCH + D_MODEL
C_GSC = C_GSSD + D_MODEL
C_DT = C_GSC + D_MODEL

FF_CHUNK = 256
FFN_ROWS = 512
MIX_ROWS_PROMPT = 256
MIX_BATCH_SAMPLE = 32
MIX_SUBSTEPS_SAMPLE = 8
SSD_CHUNK = 128
VMEM_LIMIT = 56 * 1024 * 1024


def _dot(a, b):
    return jnp.dot(a, b, preferred_element_type=F32)


def _dot_nt(a, b):
    return lax.dot_general(a, b, (((1,), (1,)), ((), ())), preferred_element_type=F32)


def _dot_tn(a, b):
    return lax.dot_general(a, b, (((0,), (0,)), ((), ())), preferred_element_type=F32)


def _dot_hi_mid_rhs(x, sel):
    hi = x.astype(BF16)
    mid = (x - hi.astype(F32)).astype(BF16)
    return _dot(hi, sel) + _dot(mid, sel)


def _dot_exact_lhs(sel, x):
    hi = x.astype(BF16)
    r1 = x - hi.astype(F32)
    mid = r1.astype(BF16)
    lo = (r1 - mid.astype(F32)).astype(BF16)
    return _dot(sel, hi) + _dot(sel, mid) + _dot(sel, lo)


def _silu(x):
    return x / (1.0 + jnp.exp(-x))


def _sigmoid(x):
    return 1.0 / (1.0 + jnp.exp(-x))


def _softplus(x):
    return jnp.maximum(x, 0.0) + jnp.log1p(jnp.exp(-jnp.abs(x)))


def _rmsnorm(x, g):
    r = lax.rsqrt(jnp.mean(x * x, axis=-1, keepdims=True) + EPS)
    return (x * r) * g


def _layer_spec(shape, layer):
    nd = len(shape) - 1
    return pl.BlockSpec((1,) + tuple(shape[1:]), lambda *_: (layer,) + (0,) * nd,
                        pipeline_mode=pl.Buffered(1))


def _mod_spec(nb, row0, layer, k):
    return pl.BlockSpec((1, nb, 1, D_MODEL), lambda b, i: (layer, row0 // nb + b, 0, k))


def _mod_body(c_ref, w_ref, b_ref, o_ref):
    sc = _silu(c_ref[...]).astype(BF16)
    o_ref[0] = _dot(sc, w_ref[0].astype(BF16)) + b_ref[0]


def _mod_call(c_all, w_ada, b_ada):
    depth, d, n = w_ada.shape
    m = c_all.shape[0]
    tn = D_MODEL
    return pl.pallas_call(
        _mod_body,
        grid=(depth, n // tn),
        in_specs=[
            pl.BlockSpec((m, d), lambda l, j: (0, 0)),
            pl.BlockSpec((1, d, tn), lambda l, j: (l, 0, j)),
            pl.BlockSpec((1, 1, tn), lambda l, j: (l, 0, j)),
        ],
        out_specs=pl.BlockSpec((1, m, tn), lambda l, j: (l, 0, j)),
        out_shape=jax.ShapeDtypeStruct((depth, m, n), F32),
        compiler_params=pltpu.CompilerParams(
            dimension_semantics=("arbitrary", "arbitrary"),
            vmem_limit_bytes=VMEM_LIMIT),
        name="adaln_mod",
    )(c_all, w_ada, b_ada.reshape(depth, 1, n))


def _ffn_body(x_ref, sh_ref, sc_ref, gt_ref, nw_ref, wgu_ref, wd_ref, nf_ref,
              o_ref, act_ref, *, final_norm):
    x = x_ref[...]
    nb, lb, d = x.shape
    h = _rmsnorm(x, nw_ref[0]) * (1.0 + sc_ref[0]) + sh_ref[0]
    hb = h.reshape(nb * lb, d).astype(BF16)
    for c in range(D_FF // FF_CHUNK):
        lo = c * FF_CHUNK
        g = _dot(hb, wgu_ref[0, :, lo:lo + FF_CHUNK])
        u = _dot(hb, wgu_ref[0, :, D_FF + lo:D_FF + lo + FF_CHUNK])
        act_ref[:, lo:lo + FF_CHUNK] = (_silu(g) * u).astype(BF16)
    y = _dot(act_ref[...], wd_ref[0])
    out = x + (0.5 * gt_ref[0]) * y.reshape(nb, lb, d)
    if final_norm:
        out = _rmsnorm(out, nf_ref[...])
    o_ref[...] = out


def _ffn_call(x, mod, row0, k_mod, layer, norm_w, w_gu, w_down, norm_final, *, nb, lb,
              final_norm):
    NB, LB, d = x.shape
    rows = nb * lb
    return pl.pallas_call(
        functools.partial(_ffn_body, final_norm=final_norm),
        grid=(NB // nb, LB // lb),
        in_specs=[
            pl.BlockSpec((nb, lb, d), lambda b, i: (b, i, 0)),
            _mod_spec(nb, row0, layer, k_mod),
            _mod_spec(nb, row0, layer, k_mod + 1),
            _mod_spec(nb, row0, layer, k_mod + 2),
            _layer_spec(norm_w.shape, layer),
            _layer_spec(w_gu.shape, layer),
            _layer_spec(w_down.shape, layer),
            pl.BlockSpec((1, d), lambda b, i: (0, 0)),
        ],
        out_specs=pl.BlockSpec((nb, lb, d), lambda b, i: (b, i, 0)),
        out_shape=jax.ShapeDtypeStruct(x.shape, F32),
        scratch_shapes=[pltpu.VMEM((rows, D_FF), BF16)],
        compiler_params=pltpu.CompilerParams(
            dimension_semantics=("arbitrary", "arbitrary"),
            vmem_limit_bytes=VMEM_LIMIT),
        name="ffn",
    )(x, mod, mod, mod, norm_w, w_gu, w_down, norm_final.reshape(1, d))


def _causal_conv(ext_ref, cur, w_ref, new_buf_ref, lb, k_w, c0):
    cols = slice(c0, c0 + cur.shape[-1])
    ext_ref[:, HALO:HALO + lb, cols] = cur
    first = HALO - (k_w - 1)
    acc = None
    for k in range(k_w):
        tap = ext_ref[:, first + k:first + k + lb, cols] * w_ref[0, k:k + 1, cols]
        acc = tap if acc is None else acc + tap
    new_buf_ref[0, :, :, cols] = ext_ref[:, lb + first:lb + HALO, cols]
    ext_ref[:, 0:HALO, cols] = ext_ref[:, lb:lb + HALO, cols]
    return acc


def _mixer_body(x_ref, sh_ref, sc_ref, gt_ref, sin_ref, bxin_ref, bsin_ref,
                nw_ref, wzx_ref, wrest_ref, wdt_ref, cw_ref, cb_ref, dtb_ref, alog_ref,
                dexp_ref, snorm_ref, wos_ref, scw_ref, wosc_ref, wo_ref, *rest,
                nb, lb, chunk, nsub, n_alias):
    rest = rest[n_alias:]
    (o_ref, sout_ref, bxout_ref, bsout_ref,
     extx_ref, exts_ref, xbc_ref, y_ref, dtda_ref, side_ref) = rest
    rows = nb * lb
    sub_rows = rows // nsub
    seg = min(lb, chunk)
    nseg = chunk // seg
    seg_shift = seg.bit_length() - 1
    assert seg == 1 << seg_shift and sub_rows % chunk == 0 and rows % nsub == 0
    half = D_MODEL // 2
    single = nsub == 1
    t = pl.program_id(1)
    base = 0 if single else pl.multiple_of(lax.rem(t, nsub) * sub_rows, sub_rows)
    env = {}

    @pl.when(t < nsub)
    def _():
        sout_ref[0] = sin_ref[0]

    @pl.when(t == 0)
    def _():
        extx_ref[:, HALO - (SSD_CONV - 1):HALO, :] = bxin_ref[0]
        exts_ref[:, HALO - (SC_CONV - 1):HALO, :] = bsin_ref[0]

    def proj(lo, width):
        if lo >= C_DT:
            w_ref, lo = wdt_ref, lo - C_DT
        elif lo >= C_SCB:
            w_ref, lo = wrest_ref, lo - C_SCB
        else:
            w_ref = wzx_ref
        return _dot(env["hb"], w_ref[0, :, lo:lo + width])

    def put_side(k, hf, val):
        if single:
            env["side", k, hf] = val
        else:
            side_ref[:, k * D_MODEL + hf * half:k * D_MODEL + (hf + 1) * half] = val

    def get_side(k):
        if single:
            return jnp.concatenate([env["side", k, 0], env["side", k, 1]], axis=1)
        return side_ref[:, k * D_MODEL:(k + 1) * D_MODEL]

    def rows_of(name, col, r0):
        if single:
            return env[name][r0:r0 + chunk]
        return dtda_ref[pl.ds(base + r0, chunk), col * LANES:(col + 1) * LANES]

    def side_scc(hf):
        env["scc", hf] = proj(C_SCC + hf * half, half)

    def side_v(hf):
        env["v", hf] = env.pop(("scc", hf)) * proj(C_SCH + hf * half, half)

    def side_conv(_):
        v = jnp.concatenate([env.pop(("v", 0)), env.pop(("v", 1))], axis=1)
        env["u"] = _causal_conv(exts_ref, v.reshape(nb, lb, D_MODEL), scw_ref, bsout_ref, lb,
                                SC_CONV, 0).reshape(rows, D_MODEL)

    def side_su(hf):
        lo = hf * half
        env["su", hf] = (proj(C_SCB + lo, half) * env["u"][:, lo:lo + half]).astype(BF16)

    def side_ysc(hf):
        if hf == 0:
            env["su"] = jnp.concatenate([env.pop(("su", 0)), env.pop(("su", 1))], axis=1)
        env["ysc", hf] = _dot(env["su"], wosc_ref[0, :, hf * half:(hf + 1) * half])

    def side_gsc(hf):
        put_side(2, hf, _sigmoid(proj(C_GSC + hf * half, half)) * env.pop(("ysc", hf)))

    def side_gssd(hf):
        put_side(1, hf, _sigmoid(proj(C_GSSD + hf * half, half)))

    def side_z(hf):
        put_side(0, hf, _silu(proj(C_Z + hf * half, half)))

    queue = [functools.partial(f, hf) for f, hf in (
        (side_scc, 0), (side_v, 0), (side_scc, 1), (side_v, 1), (side_conv, 0),
        (side_su, 0), (side_su, 1), (side_ysc, 0), (side_ysc, 1), (side_gsc, 0), (side_gsc, 1),
        (side_gssd, 0), (side_gssd, 1), (side_z, 0), (side_z, 1))]

    def side_step():
        if queue:
            queue.pop(0)()

    def before_scan():
        x = x_ref[...]
        h = _rmsnorm(x, nw_ref[0]) * (1.0 + sc_ref[0]) + sh_ref[0]
        env["hb"] = h.reshape(rows, D_MODEL).astype(BF16)

        for c0 in range(0, D_XBC, GROUP_WIDTH):
            piece = proj(C_XBC + c0, GROUP_WIDTH).reshape(nb, lb, GROUP_WIDTH)
            piece = _causal_conv(extx_ref, piece, cw_ref, bxout_ref, lb, SSD_CONV, c0)
            piece = _silu(piece + cb_ref[0, :, c0:c0 + GROUP_WIDTH])
            xbc_ref[:, c0:c0 + GROUP_WIDTH] = piece.reshape(rows, GROUP_WIDTH)
            side_step()

        dt_all = _softplus(proj(C_DT, DT_PAD) + dtb_ref[0])
        da_all = dt_all * (-jnp.exp(alog_ref[0]))
        if single:
            env["dt"], env["da"] = dt_all, da_all
        else:
            dtda_ref[:, 0:LANES] = dt_all
            dtda_ref[:, LANES:2 * LANES] = da_all
            while queue:
                side_step()

    def scan():
        ri = lax.broadcasted_iota(jnp.int32, (chunk, chunk), 0)
        ci = lax.broadcasted_iota(jnp.int32, (chunk, chunk), 1)
        same_seq = (ri >> seg_shift) == (ci >> seg_shift)
        causal = jnp.logical_and(same_seq, ci <= ri)
        causal_sel = causal.astype(F32).astype(BF16)
        seq_sel = same_seq.astype(F32).astype(BF16)
        eh = lax.broadcasted_iota(jnp.int32, (LANES, SSD_INNER), 0)
        ej = lax.broadcasted_iota(jnp.int32, (LANES, SSD_INNER), 1)
        head_sel = (eh == ej // SSD_HEAD_DIM).astype(F32).astype(BF16)
        lane_head = lax.broadcasted_iota(jnp.int32, (chunk, GROUP_WIDTH), 1) // SSD_HEAD_DIM
        bc0 = SSD_INNER
        cc0 = SSD_INNER + SSD_GROUPS * SSD_STATE

        for c in range(sub_rows // chunk):
            r0 = c * chunk
            rsl = pl.ds(base + r0, chunk)
            xs = xbc_ref[rsl, 0:SSD_INNER]
            dt = rows_of("dt", 0, r0)
            da = rows_of("da", 1, r0)
            acum = _dot_exact_lhs(causal_sel, da)
            if nseg == 1:
                atot = acum[chunk - 1:chunk, :]
            else:
                atot = _dot_exact_lhs(seq_sel, da)
            ex = _dot_hi_mid_rhs(
                jnp.concatenate([dt * jnp.exp(atot - acum), jnp.exp(acum)], axis=0), head_sel)
            xd = xs * ex[0:chunk]
            e_a = ex[chunk:2 * chunk]
            xs_b = xs.astype(BF16)
            acum_t = acum.T
            dt_t = dt.T
            side_step()

            for g in range(SSD_GROUPS):
                gl = g * GROUP_WIDTH
                bg = xbc_ref[rsl, bc0 + g * SSD_STATE:bc0 + (g + 1) * SSD_STATE]
                cg = xbc_ref[rsl, cc0 + g * SSD_STATE:cc0 + (g + 1) * SSD_STATE]
                cbm = _dot_nt(cg.astype(BF16), bg.astype(BF16))
                ms = []
                for r in range(HEADS_PER_GROUP):
                    hh = g * HEADS_PER_GROUP + r
                    sgm = acum[:, hh:hh + 1] - acum_t[hh:hh + 1, :]
                    dec = jnp.exp(jnp.where(causal, sgm, -jnp.inf))
                    ms.append((cbm * dec * dt_t[hh:hh + 1, :]).astype(BF16))
                mg = jnp.concatenate(ms, axis=1)
                xg = xs_b[:, gl:gl + GROUP_WIDTH]
                rhs = jnp.concatenate(
                    [jnp.where(lane_head == r, xg, jnp.zeros_like(xg))
                     for r in range(HEADS_PER_GROUP)], axis=0)
                y_diag = _dot(mg, rhs)

                for j in range(nseg):
                    q0 = j * seg
                    b_loc = (r0 + q0) // lb
                    s_old = sout_ref[0, b_loc, gl:gl + GROUP_WIDTH, :]
                    y_off = _dot_nt(cg[q0:q0 + seg].astype(BF16), s_old.astype(BF16))
                    y_ref[pl.ds(base + r0 + q0, seg), gl:gl + GROUP_WIDTH] = (
                        y_diag[q0:q0 + seg] + y_off * e_a[q0:q0 + seg, gl:gl + GROUP_WIDTH])
                    s_add = _dot_tn(xd[q0:q0 + seg, gl:gl + GROUP_WIDTH].astype(BF16),
                                    bg[q0:q0 + seg].astype(BF16))
                    q_last = q0 + seg - 1 if nseg > 1 else 0
                    for r in range(HEADS_PER_GROUP):
                        hh = g * HEADS_PER_GROUP + r
                        keep = jnp.exp(atot[q_last:q_last + 1, hh:hh + 1])
                        p0 = r * SSD_HEAD_DIM
                        sout_ref[0, b_loc, gl + p0:gl + p0 + SSD_HEAD_DIM, :] = (
                            keep * s_old[p0:p0 + SSD_HEAD_DIM] + s_add[p0:p0 + SSD_HEAD_DIM])
                side_step()

    def after_scan():
        while queue:
            side_step()
        y = (y_ref[...] + xbc_ref[:, 0:SSD_INNER] * dexp_ref[0]) * get_side(0)
        parts = []
        for g in range(SSD_GROUPS):
            yg = y[:, g * GROUP_WIDTH:(g + 1) * GROUP_WIDTH]
            parts.append(yg * lax.rsqrt(jnp.mean(yg * yg, axis=-1, keepdims=True) + EPS))
        yn = jnp.concatenate(parts, axis=1) * snorm_ref[0]
        y_ssd = _dot(yn.astype(BF16), wos_ref[0])
        merged = get_side(1) * y_ssd + get_side(2)
        out = _dot(merged.astype(BF16), wo_ref[0]).reshape(nb, lb, D_MODEL)
        o_ref[...] = x_ref[...] + gt_ref[0] * out

    if single:
        before_scan()
        scan()
        after_scan()
    else:
        sub = lax.rem(t, nsub)
        pl.when(sub == 0)(before_scan)
        scan()
        pl.when(sub == nsub - 1)(after_scan)


def _mixer_call(x, mod, row0, layer, state_layer, s_in, bx_in, bs_in, lw, prev, *, nb, lb,
                chunk, nsub):
    NB, LB, d = x.shape
    rows = nb * lb
    sb = nb // nsub
    depth = lw["w_in_zx"].shape[0]
    assert nsub == 1 or LB == lb

    def per_tile(arr, lead):
        return pl.BlockSpec((1, nb) + arr.shape[2:], lambda b, t: (lead, b, 0, 0))

    def per_step(arr, lead):
        return pl.BlockSpec((1, sb) + arr.shape[2:],
                            lambda b, t: (lead, b * nsub + t % nsub, 0, 0))

    consts = [lw["norm_mix"], lw["w_in_zx"], lw["w_in_rest"], lw["w_in_dt"], lw["ssd_conv_w"],
              lw["ssd_conv_b"], lw["dt_bias"], lw["a_log"], lw["d_exp"], lw["ssd_norm"],
              lw["w_out_ssd"], lw["sc_conv_w"], lw["w_out_sc"], lw["w_o"]]
    state_shapes = [(depth,) + a.shape[1:] for a in (s_in, bx_in, bs_in)]
    n_in = 7 + len(consts)
    prev = list(prev) if prev is not None else []
    x_spec = pl.BlockSpec((nb, lb, d), lambda b, t: (b, t // nsub, 0))
    handover_rows = rows if nsub > 1 else SUBLANES
    return pl.pallas_call(
        functools.partial(_mixer_body, nb=nb, lb=lb, chunk=chunk, nsub=nsub, n_alias=len(prev)),
        grid=(NB // nb, (LB // lb) * nsub),
        in_specs=[
            x_spec,
            _mod_spec(nb, row0, layer, 3), _mod_spec(nb, row0, layer, 4),
            _mod_spec(nb, row0, layer, 5),
            per_step(s_in, state_layer), per_tile(bx_in, state_layer),
            per_tile(bs_in, state_layer),
        ] + [_layer_spec(w.shape, layer) for w in consts]
          + [pl.BlockSpec(memory_space=pl.ANY) for _ in prev],
        out_specs=[
            x_spec, per_step(s_in, layer), per_tile(bx_in, layer), per_tile(bs_in, layer),
        ],
        out_shape=[jax.ShapeDtypeStruct(x.shape, F32)]
                  + [jax.ShapeDtypeStruct(s, F32) for s in state_shapes],
        input_output_aliases={n_in + k: 1 + k for k in range(len(prev))},
        scratch_shapes=[
            pltpu.VMEM((nb, HALO + lb, D_XBC), F32),
            pltpu.VMEM((nb, HALO + lb, D_MODEL), F32),
            pltpu.VMEM((rows, D_XBC), F32),
            pltpu.VMEM((rows, SSD_INNER), F32),
            pltpu.VMEM((handover_rows, 2 * LANES), F32),
            pltpu.VMEM((handover_rows, 3 * D_MODEL), F32),
        ],
        compiler_params=pltpu.CompilerParams(
            dimension_semantics=("arbitrary", "arbitrary"),
            vmem_limit_bytes=VMEM_LIMIT),
        name="mixer",
    )(x, mod, mod, mod, s_in, bx_in, bs_in, *consts, *prev)


def _mixer_weights(w_in, norm_mix, ssd_conv_w, ssd_conv_b, ssd_dt_bias, ssd_a_log, ssd_d,
                   ssd_norm, w_out_ssd, sc_conv_w, w_out_sc, w_o):
    depth, d, _ = w_in.shape
    dt_lo = SSD_INNER + D_XBC
    dt_hi = dt_lo + SSD_HEADS
    pad = lambda v: jnp.pad(v, ((0, 0), (0, DT_PAD - SSD_HEADS))).reshape(depth, 1, DT_PAD)
    return {
        "norm_mix": norm_mix.reshape(depth, 1, d),
        "w_in_zx": w_in[:, :, :dt_lo].astype(BF16),
        "w_in_rest": w_in[:, :, dt_hi:].astype(BF16),
        "w_in_dt": jnp.pad(w_in[:, :, dt_lo:dt_hi].astype(BF16),
                           ((0, 0), (0, 0), (0, DT_PAD - SSD_HEADS))),
        "ssd_conv_w": ssd_conv_w,
        "ssd_conv_b": ssd_conv_b.reshape(depth, 1, D_XBC),
        "dt_bias": pad(ssd_dt_bias),
        "a_log": pad(ssd_a_log),
        "d_exp": jnp.repeat(ssd_d, SSD_HEAD_DIM, axis=1).reshape(depth, 1, SSD_INNER),
        "ssd_norm": ssd_norm.reshape(depth, 1, SSD_INNER),
        "w_out_ssd": w_out_ssd.astype(BF16),
        "sc_conv_w": sc_conv_w,
        "w_out_sc": w_out_sc.astype(BF16),
        "w_o": w_o.astype(BF16),
    }


def kernel(x_prompt, x_sample, c_prompt, c_sample, state_ssm, state_conv_ssd, state_conv_short, w_ada, b_ada, norm_ffn1, norm_mix, norm_ffn2, ffn1_w_gu, ffn1_w_down, ffn2_w_gu, ffn2_w_down, w_in, ssd_conv_w, ssd_conv_b, ssd_dt_bias, ssd_a_log, ssd_d, ssd_norm, w_out_ssd, sc_conv_w, w_out_sc, w_o, norm_final):
    depth = w_ada.shape[0]
    bp, lp, d = x_prompt.shape
    bs, ls, _ = x_sample.shape
    hp = SSD_HEADS * SSD_HEAD_DIM

    mod = _mod_call(jnp.concatenate([c_sample, c_prompt], axis=0), w_ada, b_ada)
    mod = mod.reshape(depth, bs + bp, 1, N_MOD * d)
    row_s, row_p = 0, bs

    zeros_s = jnp.zeros((1, bp, hp, SSD_STATE), F32)
    zeros_bx = jnp.zeros((1, bp, SSD_CONV - 1, D_XBC), F32)
    zeros_bs = jnp.zeros((1, bp, SC_CONV - 1, D_MODEL), F32)
    state_s = state_ssm.reshape(depth, bs, hp, SSD_STATE)

    ffn_p = dict(nb=1, lb=min(FFN_ROWS, lp))
    ffn_s = dict(nb=min(FFN_ROWS // ls, bs), lb=ls)
    mix_p = dict(nb=1, lb=min(MIX_ROWS_PROMPT, lp), chunk=min(SSD_CHUNK, lp), nsub=1)
    nb_s = min(MIX_BATCH_SAMPLE, bs)
    nsub_s = min(MIX_SUBSTEPS_SAMPLE, nb_s)
    mix_s = dict(nb=nb_s, lb=ls, chunk=nb_s // nsub_s * ls, nsub=nsub_s)

    n1 = norm_ffn1.reshape(depth, 1, d)
    n2 = norm_ffn2.reshape(depth, 1, d)
    gu1, dn1 = ffn1_w_gu.astype(BF16), ffn1_w_down.astype(BF16)
    gu2, dn2 = ffn2_w_gu.astype(BF16), ffn2_w_down.astype(BF16)
    lw = _mixer_weights(w_in, norm_mix, ssd_conv_w, ssd_conv_b, ssd_dt_bias, ssd_a_log, ssd_d,
                        ssd_norm, w_out_ssd, sc_conv_w, w_out_sc, w_o)

    xp, xs = x_prompt, x_sample
    st_p = st_s = None
    for l in range(depth):
        last = l == depth - 1
        xp = _ffn_call(xp, mod, row_p, 0, l, n1, gu1, dn1, norm_final, final_norm=False, **ffn_p)
        xs = _ffn_call(xs, mod, row_s, 0, l, n1, gu1, dn1, norm_final, final_norm=False, **ffn_s)

        xp, *st_p = _mixer_call(xp, mod, row_p, l, 0, zeros_s, zeros_bx, zeros_bs, lw, st_p,
                                **mix_p)
        xs, *st_s = _mixer_call(xs, mod, row_s, l, l, state_s, state_conv_ssd, state_conv_short,
                                lw, st_s, **mix_s)

        xp = _ffn_call(xp, mod, row_p, 6, l, n2, gu2, dn2, norm_final, final_norm=last, **ffn_p)
        xs = _ffn_call(xs, mod, row_s, 6, l, n2, gu2, dn2, norm_final, final_norm=last, **ffn_s)

    shp = (SSD_HEADS, SSD_HEAD_DIM, SSD_STATE)
    return (xp, xs,
            st_p[0].reshape((depth, bp) + shp), st_p[1], st_p[2],
            st_s[0].reshape((depth, bs) + shp), st_s[1], st_s[2])
```

```python
import functools

import jax
import jax.numpy as jnp
from jax import lax
from jax.experimental import pallas as pl
from jax.experimental.pallas import tpu as pltpu

F32 = jnp.float32
BF16 = jnp.bfloat16

D_MODEL = 1024
SSD_HEADS = 16
SSD_HEAD_DIM = 64
SSD_GROUPS = 4
SSD_STATE = 128
HEADS_PER_GROUP = SSD_HEADS // SSD_GROUPS
GROUP_WIDTH = HEADS_PER_GROUP * SSD_HEAD_DIM
SSD_INNER = SSD_HEADS * SSD_HEAD_DIM
D_XBC = SSD_INNER + 2 * SSD_GROUPS * SSD_STATE
SSD_CONV = 4
SC_CONV = 3
D_FF = 2816
N_MOD = 9
EPS = 1e-6

LANES = 128
SUBLANES = 8
HALO = SUBLANES
DT_PAD = LANES

C_Z = 0
C_XBC = C_Z + SSD_INNER
C_SCB = C_XBC + D_XBC
C_SCC = C_SCB + D_MODEL
C_SCH = C_SCC + D_MODEL
C_GSSD = C_SCH + D_MODEL
C_GSC = C_GSSD + D_MODEL
C_DT = C_GSC + D_MODEL

FF_CHUNK = 256
FFN_ROWS = 512
MIX_ROWS_PROMPT = 256
MIX_BATCH_SAMPLE = 32
MIX_SUBSTEPS_SAMPLE = 8
SSD_CHUNK = 128
VMEM_LIMIT = 56 * 1024 * 1024


def _dot(a, b):
    return jnp.dot(a, b, preferred_element_type=F32)


def _dot_nt(a, b):
    return lax.dot_general(a, b, (((1,), (1,)), ((), ())), preferred_element_type=F32)


def _dot_tn(a, b):
    return lax.dot_general(a, b, (((0,), (0,)), ((), ())), preferred_element_type=F32)


def _dot_hi_mid_rhs(x, sel):
    hi = x.astype(BF16)
    mid = (x - hi.astype(F32)).astype(BF16)
    return _dot(hi, sel) + _dot(mid, sel)


def _dot_exact_lhs(sel, x):
    hi = x.astype(BF16)
    r1 = x - hi.astype(F32)
    mid = r1.astype(BF16)
    lo = (r1 - mid.astype(F32)).astype(BF16)
    return _dot(sel, hi) + _dot(sel, mid) + _dot(sel, lo)


def _silu(x):
    return x / (1.0 + jnp.exp(-x))


def _sigmoid(x):
    return 1.0 / (1.0 + jnp.exp(-x))


def _softplus(x):
    return jnp.maximum(x, 0.0) + jnp.log1p(jnp.exp(-jnp.abs(x)))


def _rmsnorm(x, g):
    r = lax.rsqrt(jnp.mean(x * x, axis=-1, keepdims=True) + EPS)
    return (x * r) * g


def _layer_spec(shape, layer):
    nd = len(shape) - 1
    return pl.BlockSpec((1,) + tuple(shape[1:]), lambda *_: (layer,) + (0,) * nd,
                        pipeline_mode=pl.Buffered(1))


def _mod_spec(nb, row0, layer, k):
    assert row0 % nb == 0
    return pl.BlockSpec((1, nb, 1, D_MODEL), lambda b, i: (layer, row0 // nb + b, 0, k))


CAST_ROWS = 256


def _cast_body(w_ref, o_ref):
    o_ref[...] = w_ref[...].astype(BF16)


def _to_bf16(w):
    depth, k, n = w.shape
    tk = min(CAST_ROWS, k)
    spec = pl.BlockSpec((1, tk, n), lambda l, i: (l, i, 0))
    return pl.pallas_call(
        _cast_body, grid=(depth, k // tk), in_specs=[spec], out_specs=spec,
        out_shape=jax.ShapeDtypeStruct(w.shape, BF16),
        compiler_params=pltpu.CompilerParams(
            dimension_semantics=("arbitrary", "arbitrary"), vmem_limit_bytes=VMEM_LIMIT),
        name="cast_bf16",
    )(w)


def _split_w_in_body(w_ref, zx_ref, rest_ref, dt_ref):
    dt_lo = SSD_INNER + D_XBC
    dt_hi = dt_lo + SSD_HEADS
    w = w_ref[0]
    zx_ref[0] = w[:, :dt_lo].astype(BF16)
    rest_ref[0] = w[:, dt_hi:].astype(BF16)
    dt = w[:, dt_lo:dt_hi].astype(BF16)
    dt_ref[0] = jnp.concatenate(
        [dt, jnp.zeros((dt.shape[0], DT_PAD - SSD_HEADS), BF16)], axis=1)


def _split_w_in(w_in):
    depth, k, n = w_in.shape
    dt_lo = SSD_INNER + D_XBC
    n_rest = n - dt_lo - SSD_HEADS
    tk = min(CAST_ROWS // 2, k)

    def spec(width):
        return pl.BlockSpec((1, tk, width), lambda l, i: (l, i, 0))

    return pl.pallas_call(
        _split_w_in_body, grid=(depth, k // tk), in_specs=[spec(n)],
        out_specs=[spec(dt_lo), spec(n_rest), spec(DT_PAD)],
        out_shape=[jax.ShapeDtypeStruct((depth, k, dt_lo), BF16),
                   jax.ShapeDtypeStruct((depth, k, n_rest), BF16),
                   jax.ShapeDtypeStruct((depth, k, DT_PAD), BF16)],
        compiler_params=pltpu.CompilerParams(
            dimension_semantics=("arbitrary", "arbitrary"), vmem_limit_bytes=VMEM_LIMIT),
        name="split_w_in",
    )(w_in)


def _mod_body(c_ref, w_ref, b_ref, o_ref):
    sc = _silu(c_ref[...]).astype(BF16)
    o_ref[0] = _dot(sc, w_ref[0].astype(BF16)) + b_ref[0]


def _mod_call(c_all, w_ada, b_ada):
    depth, d, n = w_ada.shape
    m = c_all.shape[0]
    tn = D_MODEL
    return pl.pallas_call(
        _mod_body,
        grid=(depth, n // tn),
        in_specs=[
            pl.BlockSpec((m, d), lambda l, j: (0, 0)),
            pl.BlockSpec((1, d, tn), lambda l, j: (l, 0, j)),
            pl.BlockSpec((1, 1, tn), lambda l, j: (l, 0, j)),
        ],
        out_specs=pl.BlockSpec((1, m, tn), lambda l, j: (l, 0, j)),
        out_shape=jax.ShapeDtypeStruct((depth, m, n), F32),
        compiler_params=pltpu.CompilerParams(
            dimension_semantics=("arbitrary", "arbitrary"),
            vmem_limit_bytes=VMEM_LIMIT),
        name="adaln_mod",
    )(c_all, w_ada, b_ada.reshape(depth, 1, n))


def _ffn_body(x_ref, sh_ref, sc_ref, gt_ref, nw_ref, wgu_ref, wd_ref, nf_ref,
              o_ref, act_ref, *, final_norm):
    x = x_ref[...]
    nb, lb, d = x.shape
    h = _rmsnorm(x, nw_ref[0]) * (1.0 + sc_ref[0]) + sh_ref[0]
    hb = h.reshape(nb * lb, d).astype(BF16)
    for c in range(D_FF // FF_CHUNK):
        lo = c * FF_CHUNK
        g = _dot(hb, wgu_ref[0, :, lo:lo + FF_CHUNK])
        u = _dot(hb, wgu_ref[0, :, D_FF + lo:D_FF + lo + FF_CHUNK])
        act_ref[:, lo:lo + FF_CHUNK] = (_silu(g) * u).astype(BF16)
    y = _dot(act_ref[...], wd_ref[0])
    out = x + (0.5 * gt_ref[0]) * y.reshape(nb, lb, d)
    if final_norm:
        out = _rmsnorm(out, nf_ref[...])
    o_ref[...] = out


def _ffn_call(x, mod, row0, k_mod, layer, norm_w, w_gu, w_down, norm_final, *, nb, lb,
              final_norm):
    NB, LB, d = x.shape
    rows = nb * lb
    return pl.pallas_call(
        functools.partial(_ffn_body, final_norm=final_norm),
        grid=(NB // nb, LB // lb),
        in_specs=[
            pl.BlockSpec((nb, lb, d), lambda b, i: (b, i, 0)),
            _mod_spec(nb, row0, layer, k_mod),
            _mod_spec(nb, row0, layer, k_mod + 1),
            _mod_spec(nb, row0, layer, k_mod + 2),
            _layer_spec(norm_w.shape, layer),
            _layer_spec(w_gu.shape, layer),
            _layer_spec(w_down.shape, layer),
            pl.BlockSpec((1, d), lambda b, i: (0, 0)),
        ],
        out_specs=pl.BlockSpec((nb, lb, d), lambda b, i: (b, i, 0)),
        out_shape=jax.ShapeDtypeStruct(x.shape, F32),
        scratch_shapes=[pltpu.VMEM((rows, D_FF), BF16)],
        compiler_params=pltpu.CompilerParams(
            dimension_semantics=("arbitrary", "arbitrary"),
            vmem_limit_bytes=VMEM_LIMIT),
        name="ffn",
    )(x, mod, mod, mod, norm_w, w_gu, w_down, norm_final.reshape(1, d))


def _causal_conv(ext_ref, cur, w_ref, new_buf_ref, lb, k_w, c0):
    cols = slice(c0, c0 + cur.shape[-1])
    ext_ref[:, HALO:HALO + lb, cols] = cur
    first = HALO - (k_w - 1)
    acc = None
    for k in range(k_w):
        tap = ext_ref[:, first + k:first + k + lb, cols] * w_ref[0, k:k + 1, cols]
        acc = tap if acc is None else acc + tap
    new_buf_ref[0, :, :, cols] = ext_ref[:, lb + first:lb + HALO, cols]
    ext_ref[:, 0:HALO, cols] = ext_ref[:, lb:lb + HALO, cols]
    return acc


def _mixer_body(x_ref, sh_ref, sc_ref, gt_ref, sin_ref, bxin_ref, bsin_ref,
                nw_ref, wzx_ref, wrest_ref, wdt_ref, cw_ref, cb_ref, dtb_ref, alog_ref,
                dexp_ref, snorm_ref, wos_ref, scw_ref, wosc_ref, wo_ref, *rest,
                nb, lb, chunk, nsub, n_alias):
    rest = rest[n_alias:]
    (o_ref, sout_ref, bxout_ref, bsout_ref,
     extx_ref, exts_ref, xbc_ref, y_ref, dtda_ref, side_ref) = rest
    rows = nb * lb
    sub_rows = rows // nsub
    seg = min(lb, chunk)
    nseg = chunk // seg
    seg_shift = seg.bit_length() - 1
    assert seg == 1 << seg_shift and sub_rows % chunk == 0 and rows % nsub == 0
    half = D_MODEL // 2
    single = nsub == 1
    t = pl.program_id(1)
    base = 0 if single else pl.multiple_of(lax.rem(t, nsub) * sub_rows, sub_rows)
    env = {}

    @pl.when(t < nsub)
    def _():
        sout_ref[0] = sin_ref[0]

    @pl.when(t == 0)
    def _():
        extx_ref[:, HALO - (SSD_CONV - 1):HALO, :] = bxin_ref[0]
        exts_ref[:, HALO - (SC_CONV - 1):HALO, :] = bsin_ref[0]

    def proj(lo, width):
        if lo >= C_DT:
            w_ref, lo = wdt_ref, lo - C_DT
        elif lo >= C_SCB:
            w_ref, lo = wrest_ref, lo - C_SCB
        else:
            w_ref = wzx_ref
        return _dot(env["hb"], w_ref[0, :, lo:lo + width])

    def put_side(k, hf, val):
        if single:
            env["side", k, hf] = val
        else:
            side_ref[:, k * D_MODEL + hf * half:k * D_MODEL + (hf + 1) * half] = val

    def get_side(k):
        if single:
            return jnp.concatenate([env["side", k, 0], env["side", k, 1]], axis=1)
        return side_ref[:, k * D_MODEL:(k + 1) * D_MODEL]

    def rows_of(name, col, r0):
        if single:
            return env[name][r0:r0 + chunk]
        return dtda_ref[pl.ds(base + r0, chunk), col * LANES:(col + 1) * LANES]

    def side_scc(hf):
        env["scc", hf] = proj(C_SCC + hf * half, half)

    def side_v(hf):
        env["v", hf] = env.pop(("scc", hf)) * proj(C_SCH + hf * half, half)

    def side_conv(_):
        v = jnp.concatenate([env.pop(("v", 0)), env.pop(("v", 1))], axis=1)
        env["u"] = _causal_conv(exts_ref, v.reshape(nb, lb, D_MODEL), scw_ref, bsout_ref, lb,
                                SC_CONV, 0).reshape(rows, D_MODEL)

    def side_su(hf):
        lo = hf * half
        env["su", hf] = (proj(C_SCB + lo, half) * env["u"][:, lo:lo + half]).astype(BF16)

    def side_ysc(hf):
        if hf == 0:
            env["su"] = jnp.concatenate([env.pop(("su", 0)), env.pop(("su", 1))], axis=1)
        env["ysc", hf] = _dot(env["su"], wosc_ref[0, :, hf * half:(hf + 1) * half])

    def side_gsc(hf):
        put_side(2, hf, _sigmoid(proj(C_GSC + hf * half, half)) * env.pop(("ysc", hf)))

    def side_gssd(hf):
        put_side(1, hf, _sigmoid(proj(C_GSSD + hf * half, half)))

    def side_z(hf):
        put_side(0, hf, _silu(proj(C_Z + hf * half, half)))

    queue = [functools.partial(f, hf) for f, hf in (
        (side_scc, 0), (side_v, 0), (side_scc, 1), (side_v, 1), (side_conv, 0),
        (side_su, 0), (side_su, 1), (side_ysc, 0), (side_ysc, 1), (side_gsc, 0), (side_gsc, 1),
        (side_gssd, 0), (side_gssd, 1), (side_z, 0), (side_z, 1))]

    def side_step():
        if queue:
            queue.pop(0)()

    def before_scan():
        x = x_ref[...]
        h = _rmsnorm(x, nw_ref[0]) * (1.0 + sc_ref[0]) + sh_ref[0]
        env["hb"] = h.reshape(rows, D_MODEL).astype(BF16)

        for c0 in range(0, D_XBC, GROUP_WIDTH):
            piece = proj(C_XBC + c0, GROUP_WIDTH).reshape(nb, lb, GROUP_WIDTH)
            piece = _causal_conv(extx_ref, piece, cw_ref, bxout_ref, lb, SSD_CONV, c0)
            piece = _silu(piece + cb_ref[0, :, c0:c0 + GROUP_WIDTH])
            xbc_ref[:, c0:c0 + GROUP_WIDTH] = piece.reshape(rows, GROUP_WIDTH)
            side_step()

        dt_all = _softplus(proj(C_DT, DT_PAD) + dtb_ref[0])
        da_all = dt_all * (-jnp.exp(alog_ref[0]))
        if single:
            env["dt"], env["da"] = dt_all, da_all
        else:
            dtda_ref[:, 0:LANES] = dt_all
            dtda_ref[:, LANES:2 * LANES] = da_all
            while queue:
                side_step()

    def scan():
        ri = lax.broadcasted_iota(jnp.int32, (chunk, chunk), 0)
        ci = lax.broadcasted_iota(jnp.int32, (chunk, chunk), 1)
        same_seq = (ri >> seg_shift) == (ci >> seg_shift)
        causal = jnp.logical_and(same_seq, ci <= ri)
        causal_sel = causal.astype(F32).astype(BF16)
        seq_sel = same_seq.astype(F32).astype(BF16)
        eh = lax.broadcasted_iota(jnp.int32, (LANES, SSD_INNER), 0)
        ej = lax.broadcasted_iota(jnp.int32, (LANES, SSD_INNER), 1)
        head_sel = (eh == ej // SSD_HEAD_DIM).astype(F32).astype(BF16)
        lane_head = lax.broadcasted_iota(jnp.int32, (chunk, GROUP_WIDTH), 1) // SSD_HEAD_DIM
        bc0 = SSD_INNER
        cc0 = SSD_INNER + SSD_GROUPS * SSD_STATE

        for c in range(sub_rows // chunk):
            r0 = c * chunk
            rsl = pl.ds(base + r0, chunk)
            xs = xbc_ref[rsl, 0:SSD_INNER]
            dt = rows_of("dt", 0, r0)
            da = rows_of("da", 1, r0)
            acum = _dot_exact_lhs(causal_sel, da)
            if nseg == 1:
                atot = acum[chunk - 1:chunk, :]
            else:
                atot = _dot_exact_lhs(seq_sel, da)
            ex = _dot_hi_mid_rhs(
                jnp.concatenate([dt * jnp.exp(atot - acum), jnp.exp(acum)], axis=0), head_sel)
            xd = xs * ex[0:chunk]
            e_a = ex[chunk:2 * chunk]
            xs_b = xs.astype(BF16)
            acum_t = acum.T
            dt_t = dt.T
            side_step()

            for g in range(SSD_GROUPS):
                gl = g * GROUP_WIDTH
                bg = xbc_ref[rsl, bc0 + g * SSD_STATE:bc0 + (g + 1) * SSD_STATE]
                cg = xbc_ref[rsl, cc0 + g * SSD_STATE:cc0 + (g + 1) * SSD_STATE]
                cbm = _dot_nt(cg.astype(BF16), bg.astype(BF16))
                ms = []
                for r in range(HEADS_PER_GROUP):
                    hh = g * HEADS_PER_GROUP + r
                    sgm = acum[:, hh:hh + 1] - acum_t[hh:hh + 1, :]
                    dec = jnp.exp(jnp.where(causal, sgm, -jnp.inf))
                    ms.append((cbm * dec * dt_t[hh:hh + 1, :]).astype(BF16))
                mg = jnp.concatenate(ms, axis=1)
                xg = xs_b[:, gl:gl + GROUP_WIDTH]
                rhs = jnp.concatenate(
                    [jnp.where(lane_head == r, xg, jnp.zeros_like(xg))
                     for r in range(HEADS_PER_GROUP)], axis=0)
                y_diag = _dot(mg, rhs)

                for j in range(nseg):
                    q0 = j * seg
                    b_loc = (r0 + q0) // lb
                    s_old = sout_ref[0, b_loc, gl:gl + GROUP_WIDTH, :]
                    y_off = _dot_nt(cg[q0:q0 + seg].astype(BF16), s_old.astype(BF16))
                    y_ref[pl.ds(base + r0 + q0, seg), gl:gl + GROUP_WIDTH] = (
                        y_diag[q0:q0 + seg] + y_off * e_a[q0:q0 + seg, gl:gl + GROUP_WIDTH])
                    s_add = _dot_tn(xd[q0:q0 + seg, gl:gl + GROUP_WIDTH].astype(BF16),
                                    bg[q0:q0 + seg].astype(BF16))
                    q_last = q0 + seg - 1 if nseg > 1 else 0
                    for r in range(HEADS_PER_GROUP):
                        hh = g * HEADS_PER_GROUP + r
                        keep = jnp.exp(atot[q_last:q_last + 1, hh:hh + 1])
                        p0 = r * SSD_HEAD_DIM
                        sout_ref[0, b_loc, gl + p0:gl + p0 + SSD_HEAD_DIM, :] = (
                            keep * s_old[p0:p0 + SSD_HEAD_DIM] + s_add[p0:p0 + SSD_HEAD_DIM])
                side_step()

    def after_scan():
        while queue:
            side_step()
        y = (y_ref[...] + xbc_ref[:, 0:SSD_INNER] * dexp_ref[0]) * get_side(0)
        parts = []
        for g in range(SSD_GROUPS):
            yg = y[:, g * GROUP_WIDTH:(g + 1) * GROUP_WIDTH]
            parts.append(yg * lax.rsqrt(jnp.mean(yg * yg, axis=-1, keepdims=True) + EPS))
        yn = jnp.concatenate(parts, axis=1) * snorm_ref[0]
        y_ssd = _dot(yn.astype(BF16), wos_ref[0])
        merged = get_side(1) * y_ssd + get_side(2)
        out = _dot(merged.astype(BF16), wo_ref[0]).reshape(nb, lb, D_MODEL)
        o_ref[...] = x_ref[...] + gt_ref[0] * out

    if single:
        before_scan()
        scan()
        after_scan()
    else:
        sub = lax.rem(t, nsub)
        pl.when(sub == 0)(before_scan)
        scan()
        pl.when(sub == nsub - 1)(after_scan)


def _mixer_call(x, mod, row0, layer, state_layer, s_in, bx_in, bs_in, lw, prev, *, nb, lb,
                chunk, nsub):
    NB, LB, d = x.shape
    rows = nb * lb
    sb = nb // nsub
    depth = lw["w_in_zx"].shape[0]
    assert nsub == 1 or LB == lb

    def per_tile(arr, lead):
        return pl.BlockSpec((1, nb) + arr.shape[2:], lambda b, t: (lead, b, 0, 0))

    def per_step(arr, lead):
        return pl.BlockSpec((1, sb) + arr.shape[2:],
                            lambda b, t: (lead, b * nsub + t % nsub, 0, 0))

    consts = [lw["norm_mix"], lw["w_in_zx"], lw["w_in_rest"], lw["w_in_dt"], lw["ssd_conv_w"],
              lw["ssd_conv_b"], lw["dt_bias"], lw["a_log"], lw["d_exp"], lw["ssd_norm"],
              lw["w_out_ssd"], lw["sc_conv_w"], lw["w_out_sc"], lw["w_o"]]
    state_shapes = [(depth,) + a.shape[1:] for a in (s_in, bx_in, bs_in)]
    n_in = 7 + len(consts)
    prev = list(prev) if prev is not None else []
    x_spec = pl.BlockSpec((nb, lb, d), lambda b, t: (b, t // nsub, 0))
    handover_rows = rows if nsub > 1 else SUBLANES
    return pl.pallas_call(
        functools.partial(_mixer_body, nb=nb, lb=lb, chunk=chunk, nsub=nsub, n_alias=len(prev)),
        grid=(NB // nb, (LB // lb) * nsub),
        in_specs=[
            x_spec,
            _mod_spec(nb, row0, layer, 3), _mod_spec(nb, row0, layer, 4),
            _mod_spec(nb, row0, layer, 5),
            per_step(s_in, state_layer), per_tile(bx_in, state_layer),
            per_tile(bs_in, state_layer),
        ] + [_layer_spec(w.shape, layer) for w in consts]
          + [pl.BlockSpec(memory_space=pl.ANY) for _ in prev],
        out_specs=[
            x_spec, per_step(s_in, layer), per_tile(bx_in, layer), per_tile(bs_in, layer),
        ],
        out_shape=[jax.ShapeDtypeStruct(x.shape, F32)]
                  + [jax.ShapeDtypeStruct(s, F32) for s in state_shapes],
        input_output_aliases={n_in + k: 1 + k for k in range(len(prev))},
        scratch_shapes=[
            pltpu.VMEM((nb, HALO + lb, D_XBC), F32),
            pltpu.VMEM((nb, HALO + lb, D_MODEL), F32),
            pltpu.VMEM((rows, D_XBC), F32),
            pltpu.VMEM((rows, SSD_INNER), F32),
            pltpu.VMEM((handover_rows, 2 * LANES), F32),
            pltpu.VMEM((handover_rows, 3 * D_MODEL), F32),
        ],
        compiler_params=pltpu.CompilerParams(
            dimension_semantics=("arbitrary", "arbitrary"),
            vmem_limit_bytes=VMEM_LIMIT),
        name="mixer",
    )(x, mod, mod, mod, s_in, bx_in, bs_in, *consts, *prev)


def _mixer_weights(w_in, norm_mix, ssd_conv_w, ssd_conv_b, ssd_dt_bias, ssd_a_log, ssd_d,
                   ssd_norm, w_out_ssd, sc_conv_w, w_out_sc, w_o):
    depth, d, _ = w_in.shape
    pad = lambda v: jnp.pad(v, ((0, 0), (0, DT_PAD - SSD_HEADS))).reshape(depth, 1, DT_PAD)
    w_zx, w_rest, w_dt = _split_w_in(w_in)
    return {
        "norm_mix": norm_mix.reshape(depth, 1, d),
        "w_in_zx": w_zx,
        "w_in_rest": w_rest,
        "w_in_dt": w_dt,
        "ssd_conv_w": ssd_conv_w,
        "ssd_conv_b": ssd_conv_b.reshape(depth, 1, D_XBC),
        "dt_bias": pad(ssd_dt_bias),
        "a_log": pad(ssd_a_log),
        "d_exp": jnp.repeat(ssd_d, SSD_HEAD_DIM, axis=1).reshape(depth, 1, SSD_INNER),
        "ssd_norm": ssd_norm.reshape(depth, 1, SSD_INNER),
        "w_out_ssd": _to_bf16(w_out_ssd),
        "sc_conv_w": sc_conv_w,
        "w_out_sc": _to_bf16(w_out_sc),
        "w_o": _to_bf16(w_o),
    }


def kernel(x_prompt, x_sample, c_prompt, c_sample, state_ssm, state_conv_ssd, state_conv_short, w_ada, b_ada, norm_ffn1, norm_mix, norm_ffn2, ffn1_w_gu, ffn1_w_down, ffn2_w_gu, ffn2_w_down, w_in, ssd_conv_w, ssd_conv_b, ssd_dt_bias, ssd_a_log, ssd_d, ssd_norm, w_out_ssd, sc_conv_w, w_out_sc, w_o, norm_final):
    depth = w_ada.shape[0]
    bp, lp, d = x_prompt.shape
    bs, ls, _ = x_sample.shape
    hp = SSD_HEADS * SSD_HEAD_DIM

    mod = _mod_call(jnp.concatenate([c_sample, c_prompt], axis=0), w_ada, b_ada)
    mod = mod.reshape(depth, bs + bp, 1, N_MOD * d)
    row_s, row_p = 0, bs

    zeros_s = jnp.zeros((1, bp, hp, SSD_STATE), F32)
    zeros_bx = jnp.zeros((1, bp, SSD_CONV - 1, D_XBC), F32)
    zeros_bs = jnp.zeros((1, bp, SC_CONV - 1, D_MODEL), F32)
    state_s = state_ssm.reshape(depth, bs, hp, SSD_STATE)

    ffn_p = dict(nb=1, lb=min(FFN_ROWS, lp))
    ffn_s = dict(nb=min(FFN_ROWS // ls, bs), lb=ls)
    mix_p = dict(nb=1, lb=min(MIX_ROWS_PROMPT, lp), chunk=min(SSD_CHUNK, lp), nsub=1)
    nb_s = min(MIX_BATCH_SAMPLE, bs)
    nsub_s = min(MIX_SUBSTEPS_SAMPLE, nb_s)
    mix_s = dict(nb=nb_s, lb=ls, chunk=nb_s // nsub_s * ls, nsub=nsub_s)

    n1 = norm_ffn1.reshape(depth, 1, d)
    n2 = norm_ffn2.reshape(depth, 1, d)
    gu1, dn1 = _to_bf16(ffn1_w_gu), _to_bf16(ffn1_w_down)
    gu2, dn2 = _to_bf16(ffn2_w_gu), _to_bf16(ffn2_w_down)
    lw = _mixer_weights(w_in, norm_mix, ssd_conv_w, ssd_conv_b, ssd_dt_bias, ssd_a_log, ssd_d,
                        ssd_norm, w_out_ssd, sc_conv_w, w_out_sc, w_o)

    xp, xs = x_prompt, x_sample
    st_p = st_s = None
    for l in range(depth):
        last = l == depth - 1
        xp = _ffn_call(xp, mod, row_p, 0, l, n1, gu1, dn1, norm_final, final_norm=False, **ffn_p)
        xs = _ffn_call(xs, mod, row_s, 0, l, n1, gu1, dn1, norm_final, final_norm=False, **ffn_s)

        xp, *st_p = _mixer_call(xp, mod, row_p, l, 0, zeros_s, zeros_bx, zeros_bs, lw, st_p,
                                **mix_p)
        xs, *st_s = _mixer_call(xs, mod, row_s, l, l, state_s, state_conv_ssd, state_conv_short,
                                lw, st_s, **mix_s)

        xp = _ffn_call(xp, mod, row_p, 6, l, n2, gu2, dn2, norm_final, final_norm=last, **ffn_p)
        xs = _ffn_call(xs, mod, row_s, 6, l, n2, gu2, dn2, norm_final, final_norm=last, **ffn_s)

    shp = (SSD_HEADS, SSD_HEAD_DIM, SSD_STATE)
    return (xp, xs,
            st_p[0].reshape((depth, bp) + shp), st_p[1], st_p[2],
            st_s[0].reshape((depth, bs) + shp), st_s[1], st_s[2])
```

```python
import functools

import jax
import jax.numpy as jnp
from jax import lax
from jax.experimental import pallas as pl
from jax.experimental.pallas import tpu as pltpu

F32 = jnp.float32
BF16 = jnp.bfloat16

D_MODEL = 1024
SSD_HEADS = 16
SSD_HEAD_DIM = 64
SSD_GROUPS = 4
SSD_STATE = 128
HEADS_PER_GROUP = SSD_HEADS // SSD_GROUPS
GROUP_WIDTH = HEADS_PER_GROUP * SSD_HEAD_DIM
SSD_INNER = SSD_HEADS * SSD_HEAD_DIM
D_XBC = SSD_INNER + 2 * SSD_GROUPS * SSD_STATE
SSD_CONV = 4
SC_CONV = 3
D_FF = 2816
N_MOD = 9
EPS = 1e-6

LANES = 128
SUBLANES = 8
HALO = SUBLANES
DT_PAD = LANES

C_Z = 0
C_XBC = C_Z + SSD_INNER
C_SCB = C_XBC + D_XBC
C_SCC = C_SCB + D_MODEL
C_SCH = C_SCC + D_MODEL
C_GSSD = C_SCH + D_MODEL
C_GSC = C_GSSD + D_MODEL
C_DT = C_GSC + D_MODEL

FF_CHUNK = 256
FFN_ROWS = 512
MIX_ROWS_PROMPT = 256
MIX_BATCH_SAMPLE = 16
MIX_SUBSTEPS_SAMPLE = 4
SSD_CHUNK = 128
VMEM_LIMIT = 56 * 1024 * 1024


def _dot(a, b):
    return jnp.dot(a, b, preferred_element_type=F32)


def _dot_nt(a, b):
    return lax.dot_general(a, b, (((1,), (1,)), ((), ())), preferred_element_type=F32)


def _dot_tn(a, b):
    return lax.dot_general(a, b, (((0,), (0,)), ((), ())), preferred_element_type=F32)


def _dot_hi_mid_rhs(x, sel):
    hi = x.astype(BF16)
    mid = (x - hi.astype(F32)).astype(BF16)
    return _dot(hi, sel) + _dot(mid, sel)


def _dot_exact_lhs(sel, x):
    hi = x.astype(BF16)
    r1 = x - hi.astype(F32)
    mid = r1.astype(BF16)
    lo = (r1 - mid.astype(F32)).astype(BF16)
    return _dot(sel, hi) + _dot(sel, mid) + _dot(sel, lo)


def _silu(x):
    return x / (1.0 + jnp.exp(-x))


def _sigmoid(x):
    return 1.0 / (1.0 + jnp.exp(-x))


def _softplus(x):
    return jnp.maximum(x, 0.0) + jnp.log1p(jnp.exp(-jnp.abs(x)))


def _rmsnorm(x, g):
    r = lax.rsqrt(jnp.mean(x * x, axis=-1, keepdims=True) + EPS)
    return (x * r) * g


def _layer_spec(shape, layer):
    nd = len(shape) - 1
    return pl.BlockSpec((1,) + tuple(shape[1:]), lambda *_: (layer,) + (0,) * nd,
                        pipeline_mode=pl.Buffered(1))


def _mod_spec(nb, row0, layer, k):
    assert row0 % nb == 0
    return pl.BlockSpec((1, nb, 1, D_MODEL), lambda b, i: (layer, row0 // nb + b, 0, k))


def _mod_body(c_ref, w_ref, b_ref, o_ref):
    sc = _silu(c_ref[...]).astype(BF16)
    o_ref[0] = _dot(sc, w_ref[0].astype(BF16)) + b_ref[0]


def _mod_call(c_all, w_ada, b_ada):
    depth, d, n = w_ada.shape
    m = c_all.shape[0]
    tn = D_MODEL
    return pl.pallas_call(
        _mod_body,
        grid=(depth, n // tn),
        in_specs=[
            pl.BlockSpec((m, d), lambda l, j: (0, 0)),
            pl.BlockSpec((1, d, tn), lambda l, j: (l, 0, j)),
            pl.BlockSpec((1, 1, tn), lambda l, j: (l, 0, j)),
        ],
        out_specs=pl.BlockSpec((1, m, tn), lambda l, j: (l, 0, j)),
        out_shape=jax.ShapeDtypeStruct((depth, m, n), F32),
        compiler_params=pltpu.CompilerParams(
            dimension_semantics=("arbitrary", "arbitrary"),
            vmem_limit_bytes=VMEM_LIMIT),
        name="adaln_mod",
    )(c_all, w_ada, b_ada.reshape(depth, 1, n))


def _ffn_body(x_ref, xn_ref, sh_ref, sc_ref, gt_ref, shn_ref, scn_ref, nw_ref, wgu_ref, wd_ref,
              nf_ref, o_ref, act_ref, hb_ref, act0_ref, *, final_norm):
    nb, lb, d = x_ref.shape

    def normed(xr, shr, scr):
        h = _rmsnorm(xr[...], nw_ref[0]) * (1.0 + scr[0]) + shr[0]
        return h.reshape(nb * lb, d).astype(BF16)

    def act_chunk(hb, c):
        lo = c * FF_CHUNK
        g = _dot(hb, wgu_ref[0, :, lo:lo + FF_CHUNK])
        u = _dot(hb, wgu_ref[0, :, D_FF + lo:D_FF + lo + FF_CHUNK])
        return (_silu(g) * u).astype(BF16)

    @pl.when(jnp.logical_and(pl.program_id(0) == 0, pl.program_id(1) == 0))
    def _():
        hb0 = normed(x_ref, sh_ref, sc_ref)
        hb_ref[...] = hb0
        act0_ref[...] = act_chunk(hb0, 0)

    hb = hb_ref[...]
    act_ref[:, 0:FF_CHUNK] = act0_ref[...]
    n_chunks = D_FF // FF_CHUNK
    for c in range(1, n_chunks):
        act_ref[:, c * FF_CHUNK:(c + 1) * FF_CHUNK] = act_chunk(hb, c)
        if c == n_chunks // 2:
            hb_next = normed(xn_ref, shn_ref, scn_ref)
            hb_ref[...] = hb_next
            act0_ref[...] = act_chunk(hb_next, 0)
    y = _dot(act_ref[...], wd_ref[0])
    out = x_ref[...] + (0.5 * gt_ref[0]) * y.reshape(nb, lb, d)
    if final_norm:
        out = _rmsnorm(out, nf_ref[...])
    o_ref[...] = out


def _ffn_call(x, mod, row0, k_mod, layer, norm_w, w_gu, w_down, norm_final, *, nb, lb,
              final_norm):
    NB, LB, d = x.shape
    rows = nb * lb
    n_tiles = LB // lb
    last = (NB // nb) * n_tiles - 1

    def nxt(b, i):
        flat = jnp.minimum(b * n_tiles + i + 1, last)
        return flat // n_tiles, flat % n_tiles

    def mod_next(k):
        return pl.BlockSpec((1, nb, 1, d), lambda b, i: (layer, row0 // nb + nxt(b, i)[0], 0, k))

    return pl.pallas_call(
        functools.partial(_ffn_body, final_norm=final_norm),
        grid=(NB // nb, LB // lb),
        in_specs=[
            pl.BlockSpec((nb, lb, d), lambda b, i: (b, i, 0)),
            pl.BlockSpec((nb, lb, d), lambda b, i: nxt(b, i) + (0,)),
            _mod_spec(nb, row0, layer, k_mod),
            _mod_spec(nb, row0, layer, k_mod + 1),
            _mod_spec(nb, row0, layer, k_mod + 2),
            mod_next(k_mod), mod_next(k_mod + 1),
            _layer_spec(norm_w.shape, layer),
            _layer_spec(w_gu.shape, layer),
            _layer_spec(w_down.shape, layer),
            pl.BlockSpec((1, d), lambda b, i: (0, 0)),
        ],
        out_specs=pl.BlockSpec((nb, lb, d), lambda b, i: (b, i, 0)),
        out_shape=jax.ShapeDtypeStruct(x.shape, F32),
        scratch_shapes=[pltpu.VMEM((rows, D_FF), BF16), pltpu.VMEM((rows, d), BF16),
                        pltpu.VMEM((rows, FF_CHUNK), BF16)],
        compiler_params=pltpu.CompilerParams(
            dimension_semantics=("arbitrary", "arbitrary"),
            vmem_limit_bytes=VMEM_LIMIT),
        name="ffn",
    )(x, x, mod, mod, mod, mod, mod, norm_w, w_gu, w_down, norm_final.reshape(1, d))


def _causal_conv(ext_ref, cur, w_ref, new_buf_ref, lb, k_w, c0):
    cols = slice(c0, c0 + cur.shape[-1])
    ext_ref[:, HALO:HALO + lb, cols] = cur
    first = HALO - (k_w - 1)
    acc = None
    for k in range(k_w):
        tap = ext_ref[:, first + k:first + k + lb, cols] * w_ref[0, k:k + 1, cols]
        acc = tap if acc is None else acc + tap
    new_buf_ref[0, :, :, cols] = ext_ref[:, lb + first:lb + HALO, cols]
    ext_ref[:, 0:HALO, cols] = ext_ref[:, lb:lb + HALO, cols]
    return acc


def _mixer_body(x_ref, sh_ref, sc_ref, gt_ref, sin_ref, bxin_ref, bsin_ref,
                nw_ref, wzx_ref, wrest_ref, wdt_ref, cw_ref, cb_ref, dtb_ref, alog_ref,
                dexp_ref, snorm_ref, wos_ref, scw_ref, wosc_ref, wo_ref, *rest,
                nb, lb, chunk, nsub, n_alias):
    rest = rest[n_alias:]
    (o_ref, sout_ref, bxout_ref, bsout_ref,
     extx_ref, exts_ref, xbc_ref, y_ref, dtda_ref, side_ref) = rest
    rows = nb * lb
    sub_rows = rows // nsub
    seg = min(lb, chunk)
    nseg = chunk // seg
    seg_shift = seg.bit_length() - 1
    assert seg == 1 << seg_shift and sub_rows % chunk == 0 and rows % nsub == 0
    half = D_MODEL // 2
    single = nsub == 1
    t = pl.program_id(1)
    base = 0 if single else pl.multiple_of(lax.rem(t, nsub) * sub_rows, sub_rows)
    env = {}

    @pl.when(t < nsub)
    def _():
        sout_ref[0] = sin_ref[0]

    @pl.when(t == 0)
    def _():
        extx_ref[:, HALO - (SSD_CONV - 1):HALO, :] = bxin_ref[0]
        exts_ref[:, HALO - (SC_CONV - 1):HALO, :] = bsin_ref[0]

    def proj(lo, width):
        if lo >= C_DT:
            w_ref, lo = wdt_ref, lo - C_DT
        elif lo >= C_SCB:
            w_ref, lo = wrest_ref, lo - C_SCB
        else:
            w_ref = wzx_ref
        return _dot(env["hb"], w_ref[0, :, lo:lo + width])

    def put_side(k, hf, val):
        if single:
            env["side", k, hf] = val
        else:
            side_ref[:, k * D_MODEL + hf * half:k * D_MODEL + (hf + 1) * half] = val

    def get_side(k):
        if single:
            return jnp.concatenate([env["side", k, 0], env["side", k, 1]], axis=1)
        return side_ref[:, k * D_MODEL:(k + 1) * D_MODEL]

    def rows_of(name, col, r0):
        if single:
            return env[name][r0:r0 + chunk]
        return dtda_ref[pl.ds(base + r0, chunk), col * LANES:(col + 1) * LANES]

    def side_scc(hf):
        env["scc", hf] = proj(C_SCC + hf * half, half)

    def side_v(hf):
        env["v", hf] = env.pop(("scc", hf)) * proj(C_SCH + hf * half, half)

    def side_conv(_):
        v = jnp.concatenate([env.pop(("v", 0)), env.pop(("v", 1))], axis=1)
        env["u"] = _causal_conv(exts_ref, v.reshape(nb, lb, D_MODEL), scw_ref, bsout_ref, lb,
                                SC_CONV, 0).reshape(rows, D_MODEL)

    def side_su(hf):
        lo = hf * half
        env["su", hf] = (proj(C_SCB + lo, half) * env["u"][:, lo:lo + half]).astype(BF16)

    def side_ysc(hf):
        if hf == 0:
            env["su"] = jnp.concatenate([env.pop(("su", 0)), env.pop(("su", 1))], axis=1)
        env["ysc", hf] = _dot(env["su"], wosc_ref[0, :, hf * half:(hf + 1) * half])

    def side_gsc(hf):
        put_side(2, hf, _sigmoid(proj(C_GSC + hf * half, half)) * env.pop(("ysc", hf)))

    def side_gssd(hf):
        put_side(1, hf, _sigmoid(proj(C_GSSD + hf * half, half)))

    def side_z(hf):
        put_side(0, hf, _silu(proj(C_Z + hf * half, half)))

    queue = [functools.partial(f, hf) for f, hf in (
        (side_scc, 0), (side_v, 0), (side_scc, 1), (side_v, 1), (side_conv, 0),
        (side_su, 0), (side_su, 1), (side_ysc, 0), (side_ysc, 1), (side_gsc, 0), (side_gsc, 1),
        (side_gssd, 0), (side_gssd, 1), (side_z, 0), (side_z, 1))]

    def side_step():
        if queue:
            queue.pop(0)()

    ri = lax.broadcasted_iota(jnp.int32, (chunk, chunk), 0)
    ci = lax.broadcasted_iota(jnp.int32, (chunk, chunk), 1)
    same_seq = (ri >> seg_shift) == (ci >> seg_shift)
    causal = jnp.logical_and(same_seq, ci <= ri)
    causal_sel = causal.astype(F32).astype(BF16)
    seq_sel = same_seq.astype(F32).astype(BF16)
    eh = lax.broadcasted_iota(jnp.int32, (LANES, SSD_INNER), 0)
    ej = lax.broadcasted_iota(jnp.int32, (LANES, SSD_INNER), 1)
    head_sel = (eh == ej // SSD_HEAD_DIM).astype(F32).astype(BF16)
    lane_head = lax.broadcasted_iota(jnp.int32, (chunk, GROUP_WIDTH), 1) // SSD_HEAD_DIM
    bc0 = SSD_INNER
    cc0 = SSD_INNER + SSD_GROUPS * SSD_STATE
    n_chunks = sub_rows // chunk

    def normed_input():
        x = x_ref[...]
        h = _rmsnorm(x, nw_ref[0]) * (1.0 + sc_ref[0]) + sh_ref[0]
        env["hb"] = h.reshape(rows, D_MODEL).astype(BF16)

    def dt_proj():
        dt_all = _softplus(proj(C_DT, DT_PAD) + dtb_ref[0])
        da_all = dt_all * (-jnp.exp(alog_ref[0]))
        if single:
            env["dt"], env["da"] = dt_all, da_all
        else:
            dtda_ref[:, 0:LANES] = dt_all
            dtda_ref[:, LANES:2 * LANES] = da_all

    def conv_piece(c0):
        piece = proj(C_XBC + c0, GROUP_WIDTH).reshape(nb, lb, GROUP_WIDTH)
        piece = _causal_conv(extx_ref, piece, cw_ref, bxout_ref, lb, SSD_CONV, c0)
        piece = _silu(piece + cb_ref[0, :, c0:c0 + GROUP_WIDTH])
        xbc_ref[:, c0:c0 + GROUP_WIDTH] = piece.reshape(rows, GROUP_WIDTH)

    def scan_decays(c):
        r0 = c * chunk
        dt = rows_of("dt", 0, r0)
        da = rows_of("da", 1, r0)
        acum = _dot_exact_lhs(causal_sel, da)
        if nseg == 1:
            atot = acum[chunk - 1:chunk, :]
        else:
            atot = _dot_exact_lhs(seq_sel, da)
        ex = _dot_hi_mid_rhs(
            jnp.concatenate([dt * jnp.exp(atot - acum), jnp.exp(acum)], axis=0), head_sel)
        return dict(acum=acum, atot=atot, to_end=ex[0:chunk], e_a=ex[chunk:2 * chunk],
                    acum_t=acum.T, dt_t=dt.T)

    def scan_cb(c, st):
        rsl = pl.ds(base + c * chunk, chunk)
        st["bg"] = [xbc_ref[rsl, bc0 + g * SSD_STATE:bc0 + (g + 1) * SSD_STATE].astype(BF16)
                    for g in range(SSD_GROUPS)]
        st["cbm"] = [
            _dot_nt(xbc_ref[rsl, cc0 + g * SSD_STATE:cc0 + (g + 1) * SSD_STATE].astype(BF16),
                    st["bg"][g]) for g in range(SSD_GROUPS)]

    def scan_increments(c, st):
        rsl = pl.ds(base + c * chunk, chunk)
        xs = xbc_ref[rsl, 0:SSD_INNER]
        xd = xs * st["to_end"]
        st["xs_b"] = xs.astype(BF16)
        st["s_add"] = [[
            _dot_tn(xd[j * seg:(j + 1) * seg, g * GROUP_WIDTH:(g + 1) * GROUP_WIDTH].astype(BF16),
                    st["bg"][g][j * seg:(j + 1) * seg]) for j in range(nseg)]
            for g in range(SSD_GROUPS)]

    def scan_outputs(c, st, g):
        r0 = c * chunk
        rsl = pl.ds(base + r0, chunk)
        gl = g * GROUP_WIDTH
        acum, acum_t, dt_t, atot = st["acum"], st["acum_t"], st["dt_t"], st["atot"]
        cg = xbc_ref[rsl, cc0 + g * SSD_STATE:cc0 + (g + 1) * SSD_STATE]
        cbm = st["cbm"][g]
        ms = []
        for r in range(HEADS_PER_GROUP):
            hh = g * HEADS_PER_GROUP + r
            sgm = acum[:, hh:hh + 1] - acum_t[hh:hh + 1, :]
            dec = jnp.exp(jnp.where(causal, sgm, -jnp.inf))
            ms.append((cbm * dec * dt_t[hh:hh + 1, :]).astype(BF16))
        mg = jnp.concatenate(ms, axis=1)
        xg = st["xs_b"][:, gl:gl + GROUP_WIDTH]
        rhs = jnp.concatenate(
            [jnp.where(lane_head == r, xg, jnp.zeros_like(xg))
             for r in range(HEADS_PER_GROUP)], axis=0)
        y_diag = _dot(mg, rhs)

        for j in range(nseg):
            q0 = j * seg
            b_loc = (r0 + q0) // lb
            s_old = sout_ref[0, b_loc, gl:gl + GROUP_WIDTH, :]
            y_off = _dot_nt(cg[q0:q0 + seg].astype(BF16), s_old.astype(BF16))
            y_ref[pl.ds(base + r0 + q0, seg), gl:gl + GROUP_WIDTH] = (
                y_diag[q0:q0 + seg] + y_off * st["e_a"][q0:q0 + seg, gl:gl + GROUP_WIDTH])
            s_add = st["s_add"][g][j]
            q_last = q0 + seg - 1 if nseg > 1 else 0
            for r in range(HEADS_PER_GROUP):
                hh = g * HEADS_PER_GROUP + r
                keep = jnp.exp(atot[q_last:q_last + 1, hh:hh + 1])
                p0 = r * SSD_HEAD_DIM
                sout_ref[0, b_loc, gl + p0:gl + p0 + SSD_HEAD_DIM, :] = (
                    keep * s_old[p0:p0 + SSD_HEAD_DIM] + s_add[p0:p0 + SSD_HEAD_DIM])

    def before_scan():
        normed_input()
        dt_proj()
        for c0 in range(0, D_XBC, GROUP_WIDTH):
            conv_piece(c0)
            side_step()
        while queue:
            side_step()

    def scan():
        for c in range(n_chunks):
            st = scan_decays(c)
            scan_cb(c, st)
            scan_increments(c, st)
            for g in range(SSD_GROUPS):
                scan_outputs(c, st, g)

    def fused_step():
        normed_input()
        for c0 in range(0, D_XBC, GROUP_WIDTH):
            conv_piece(c0)
            side_step()
        dt_proj()
        sts = []
        for c in range(n_chunks):
            sts.append(scan_decays(c))
            scan_cb(c, sts[c])
            scan_increments(c, sts[c])
            side_step()
        for c in range(n_chunks):
            for g in range(SSD_GROUPS):
                scan_outputs(c, sts[c], g)
                side_step()
        after_scan()

    def after_scan():
        while queue:
            side_step()
        y = (y_ref[...] + xbc_ref[:, 0:SSD_INNER] * dexp_ref[0]) * get_side(0)
        parts = []
        for g in range(SSD_GROUPS):
            yg = y[:, g * GROUP_WIDTH:(g + 1) * GROUP_WIDTH]
            parts.append(yg * lax.rsqrt(jnp.mean(yg * yg, axis=-1, keepdims=True) + EPS))
        yn = jnp.concatenate(parts, axis=1) * snorm_ref[0]
        y_ssd = _dot(yn.astype(BF16), wos_ref[0])
        merged = get_side(1) * y_ssd + get_side(2)
        out = _dot(merged.astype(BF16), wo_ref[0]).reshape(nb, lb, D_MODEL)
        o_ref[...] = x_ref[...] + gt_ref[0] * out

    if single:
        fused_step()
    else:
        sub = lax.rem(t, nsub)
        pl.when(sub == 0)(before_scan)
        scan()
        pl.when(sub == nsub - 1)(after_scan)


def _mixer_call(x, mod, row0, layer, state_layer, s_in, bx_in, bs_in, lw, prev, *, nb, lb,
                chunk, nsub):
    NB, LB, d = x.shape
    rows = nb * lb
    sb = nb // nsub
    depth = lw["w_in_zx"].shape[0]
    assert nsub == 1 or LB == lb

    def per_tile(arr, lead):
        return pl.BlockSpec((1, nb) + arr.shape[2:], lambda b, t: (lead, b, 0, 0))

    def per_step(arr, lead):
        return pl.BlockSpec((1, sb) + arr.shape[2:],
                            lambda b, t: (lead, b * nsub + t % nsub, 0, 0))

    consts = [lw["norm_mix"], lw["w_in_zx"], lw["w_in_rest"], lw["w_in_dt"], lw["ssd_conv_w"],
              lw["ssd_conv_b"], lw["dt_bias"], lw["a_log"], lw["d_exp"], lw["ssd_norm"],
              lw["w_out_ssd"], lw["sc_conv_w"], lw["w_out_sc"], lw["w_o"]]
    state_shapes = [(depth,) + a.shape[1:] for a in (s_in, bx_in, bs_in)]
    n_in = 7 + len(consts)
    prev = list(prev) if prev is not None else []
    x_spec = pl.BlockSpec((nb, lb, d), lambda b, t: (b, t // nsub, 0))
    handover_rows = rows if nsub > 1 else SUBLANES
    return pl.pallas_call(
        functools.partial(_mixer_body, nb=nb, lb=lb, chunk=chunk, nsub=nsub, n_alias=len(prev)),
        grid=(NB // nb, (LB // lb) * nsub),
        in_specs=[
            x_spec,
            _mod_spec(nb, row0, layer, 3), _mod_spec(nb, row0, layer, 4),
            _mod_spec(nb, row0, layer, 5),
            per_step(s_in, state_layer), per_tile(bx_in, state_layer),
            per_tile(bs_in, state_layer),
        ] + [_layer_spec(w.shape, layer) for w in consts]
          + [pl.BlockSpec(memory_space=pl.ANY) for _ in prev],
        out_specs=[
            x_spec, per_step(s_in, layer), per_tile(bx_in, layer), per_tile(bs_in, layer),
        ],
        out_shape=[jax.ShapeDtypeStruct(x.shape, F32)]
                  + [jax.ShapeDtypeStruct(s, F32) for s in state_shapes],
        input_output_aliases={n_in + k: 1 + k for k in range(len(prev))},
        scratch_shapes=[
            pltpu.VMEM((nb, HALO + lb, D_XBC), F32),
            pltpu.VMEM((nb, HALO + lb, D_MODEL), F32),
            pltpu.VMEM((rows, D_XBC), F32),
            pltpu.VMEM((rows, SSD_INNER), F32),
            pltpu.VMEM((handover_rows, 2 * LANES), F32),
            pltpu.VMEM((handover_rows, 3 * D_MODEL), F32),
        ],
        compiler_params=pltpu.CompilerParams(
            dimension_semantics=("arbitrary", "arbitrary"),
            vmem_limit_bytes=VMEM_LIMIT),
        name="mixer",
    )(x, mod, mod, mod, s_in, bx_in, bs_in, *consts, *prev)


def _mixer_weights(w_in, norm_mix, ssd_conv_w, ssd_conv_b, ssd_dt_bias, ssd_a_log, ssd_d,
                   ssd_norm, w_out_ssd, sc_conv_w, w_out_sc, w_o):
    depth, d, _ = w_in.shape
    dt_lo = SSD_INNER + D_XBC
    dt_hi = dt_lo + SSD_HEADS
    pad = lambda v: jnp.pad(v, ((0, 0), (0, DT_PAD - SSD_HEADS))).reshape(depth, 1, DT_PAD)
    return {
        "norm_mix": norm_mix.reshape(depth, 1, d),
        "w_in_zx": w_in[:, :, :dt_lo].astype(BF16),
        "w_in_rest": w_in[:, :, dt_hi:].astype(BF16),
        "w_in_dt": jnp.pad(w_in[:, :, dt_lo:dt_hi].astype(BF16),
                           ((0, 0), (0, 0), (0, DT_PAD - SSD_HEADS))),
        "ssd_conv_w": ssd_conv_w,
        "ssd_conv_b": ssd_conv_b.reshape(depth, 1, D_XBC),
        "dt_bias": pad(ssd_dt_bias),
        "a_log": pad(ssd_a_log),
        "d_exp": jnp.repeat(ssd_d, SSD_HEAD_DIM, axis=1).reshape(depth, 1, SSD_INNER),
        "ssd_norm": ssd_norm.reshape(depth, 1, SSD_INNER),
        "w_out_ssd": w_out_ssd.astype(BF16),
        "sc_conv_w": sc_conv_w,
        "w_out_sc": w_out_sc.astype(BF16),
        "w_o": w_o.astype(BF16),
    }


def kernel(x_prompt, x_sample, c_prompt, c_sample, state_ssm, state_conv_ssd, state_conv_short, w_ada, b_ada, norm_ffn1, norm_mix, norm_ffn2, ffn1_w_gu, ffn1_w_down, ffn2_w_gu, ffn2_w_down, w_in, ssd_conv_w, ssd_conv_b, ssd_dt_bias, ssd_a_log, ssd_d, ssd_norm, w_out_ssd, sc_conv_w, w_out_sc, w_o, norm_final):
    depth = w_ada.shape[0]
    bp, lp, d = x_prompt.shape
    bs, ls, _ = x_sample.shape
    hp = SSD_HEADS * SSD_HEAD_DIM

    mod = _mod_call(jnp.concatenate([c_sample, c_prompt], axis=0), w_ada, b_ada)
    mod = mod.reshape(depth, bs + bp, 1, N_MOD * d)
    row_s, row_p = 0, bs

    zeros_s = jnp.zeros((1, bp, hp, SSD_STATE), F32)
    zeros_bx = jnp.zeros((1, bp, SSD_CONV - 1, D_XBC), F32)
    zeros_bs = jnp.zeros((1, bp, SC_CONV - 1, D_MODEL), F32)
    state_s = state_ssm.reshape(depth, bs, hp, SSD_STATE)

    ffn_p = dict(nb=1, lb=min(FFN_ROWS, lp))
    ffn_s = dict(nb=min(FFN_ROWS // ls, bs), lb=ls)
    mix_p = dict(nb=1, lb=min(MIX_ROWS_PROMPT, lp), chunk=min(SSD_CHUNK, lp), nsub=1)
    nb_s = min(MIX_BATCH_SAMPLE, bs)
    nsub_s = min(MIX_SUBSTEPS_SAMPLE, nb_s)
    mix_s = dict(nb=nb_s, lb=ls, chunk=nb_s // nsub_s * ls, nsub=nsub_s)

    n1 = norm_ffn1.reshape(depth, 1, d)
    n2 = norm_ffn2.reshape(depth, 1, d)
    gu1, dn1 = ffn1_w_gu.astype(BF16), ffn1_w_down.astype(BF16)
    gu2, dn2 = ffn2_w_gu.astype(BF16), ffn2_w_down.astype(BF16)
    lw = _mixer_weights(w_in, norm_mix, ssd_conv_w, ssd_conv_b, ssd_dt_bias, ssd_a_log, ssd_d,
                        ssd_norm, w_out_ssd, sc_conv_w, w_out_sc, w_o)

    xp, xs = x_prompt, x_sample
    st_p = st_s = None
    for l in range(depth):
        last = l == depth - 1
        xp = _ffn_call(xp, mod, row_p, 0, l, n1, gu1, dn1, norm_final, final_norm=False, **ffn_p)
        xs = _ffn_call(xs, mod, row_s, 0, l, n1, gu1, dn1, norm_final, final_norm=False, **ffn_s)

        xp, *st_p = _mixer_call(xp, mod, row_p, l, 0, zeros_s, zeros_bx, zeros_bs, lw, st_p,
                                **mix_p)
        xs, *st_s = _mixer_call(xs, mod, row_s, l, l, state_s, state_conv_ssd, state_conv_short,
                                lw, st_s, **mix_s)

        xp = _ffn_call(xp, mod, row_p, 6, l, n2, gu2, dn2, norm_final, final_norm=last, **ffn_p)
        xs = _ffn_call(xs, mod, row_s, 6, l, n2, gu2, dn2, norm_final, final_norm=last, **ffn_s)

    shp = (SSD_HEADS, SSD_HEAD_DIM, SSD_STATE)
    return (xp, xs,
            st_p[0].reshape((depth, bp) + shp), st_p[1], st_p[2],
            st_s[0].reshape((depth, bs) + shp), st_s[1], st_s[2])
```

```python
import functools

import jax
import jax.numpy as jnp
from jax import lax
from jax.experimental import pallas as pl
from jax.experimental.pallas import tpu as pltpu

F32 = jnp.float32
BF16 = jnp.bfloat16

D_MODEL = 1024
SSD_HEADS = 16
SSD_HEAD_DIM = 64
SSD_GROUPS = 4
SSD_STATE = 128
HEADS_PER_GROUP = SSD_HEADS // SSD_GROUPS
GROUP_WIDTH = HEADS_PER_GROUP * SSD_HEAD_DIM
SSD_INNER = SSD_HEADS * SSD_HEAD_DIM
D_XBC = SSD_INNER + 2 * SSD_GROUPS * SSD_STATE
SSD_CONV = 4
SC_CONV = 3
D_FF = 2816
N_MOD = 9
EPS = 1e-6

LANES = 128
SUBLANES = 8
HALO = SUBLANES
DT_PAD = LANES

C_Z = 0
C_XBC = C_Z + SSD_INNER
C_SCB = C_XBC + D_XBC
C_SCC = C_SCB + D_MODEL
C_SCH = C_SCC + D_MODEL
C_GSSD = C_SCH + D_MODEL
C_GSC = C_GSSD + D_MODEL
C_DT = C_GSC + D_MODEL

FF_CHUNK = 256
FFN_ROWS = 512
MIX_ROWS_PROMPT = 256
MIX_BATCH_SAMPLE = 16
MIX_SUBSTEPS_SAMPLE = 4
SSD_CHUNK = 128
VMEM_LIMIT = 56 * 1024 * 1024


def _dot(a, b):
    return jnp.dot(a, b, preferred_element_type=F32)


def _dot_nt(a, b):
    return lax.dot_general(a, b, (((1,), (1,)), ((), ())), preferred_element_type=F32)


def _dot_tn(a, b):
    return lax.dot_general(a, b, (((0,), (0,)), ((), ())), preferred_element_type=F32)


def _dot_hi_mid_rhs(x, sel):
    hi = x.astype(BF16)
    mid = (x - hi.astype(F32)).astype(BF16)
    return _dot(hi, sel) + _dot(mid, sel)


def _dot_exact_lhs(sel, x):
    hi = x.astype(BF16)
    r1 = x - hi.astype(F32)
    mid = r1.astype(BF16)
    lo = (r1 - mid.astype(F32)).astype(BF16)
    return _dot(sel, hi) + _dot(sel, mid) + _dot(sel, lo)


def _silu(x):
    return x / (1.0 + jnp.exp(-x))


def _sigmoid(x):
    return 1.0 / (1.0 + jnp.exp(-x))


def _softplus(x):
    return jnp.maximum(x, 0.0) + jnp.log1p(jnp.exp(-jnp.abs(x)))


def _rmsnorm(x, g):
    r = lax.rsqrt(jnp.mean(x * x, axis=-1, keepdims=True) + EPS)
    return (x * r) * g


def _layer_spec(shape, layer):
    nd = len(shape) - 1
    return pl.BlockSpec((1,) + tuple(shape[1:]), lambda *_: (layer,) + (0,) * nd,
                        pipeline_mode=pl.Buffered(1))


def _mod_spec(nb, row0, layer, k):
    assert row0 % nb == 0
    return pl.BlockSpec((1, nb, 1, D_MODEL), lambda b, i: (layer, row0 // nb + b, 0, k))


def _mod_body(c_ref, w_ref, b_ref, o_ref):
    sc = _silu(c_ref[...]).astype(BF16)
    o_ref[0, :, 0, :] = _dot(sc, w_ref[0].astype(BF16)) + b_ref[0]


def _mod_call(c_all, w_ada, b_ada):
    depth, d, n = w_ada.shape
    m = c_all.shape[0]
    tn = D_MODEL
    return pl.pallas_call(
        _mod_body,
        grid=(depth, n // tn),
        in_specs=[
            pl.BlockSpec((m, d), lambda l, j: (0, 0)),
            pl.BlockSpec((1, d, tn), lambda l, j: (l, 0, j)),
            pl.BlockSpec((1, 1, tn), lambda l, j: (l, 0, j)),
        ],
        out_specs=pl.BlockSpec((1, m, 1, tn), lambda l, j: (l, 0, 0, j)),
        out_shape=jax.ShapeDtypeStruct((depth, m, 1, n), F32),
        compiler_params=pltpu.CompilerParams(
            dimension_semantics=("arbitrary", "arbitrary"),
            vmem_limit_bytes=VMEM_LIMIT),
        name="adaln_mod",
    )(c_all, w_ada, b_ada.reshape(depth, 1, n))


def _ffn_body(x_ref, xn_ref, sh_ref, sc_ref, gt_ref, shn_ref, scn_ref, nw_ref, wgu_ref, wd_ref,
              nf_ref, o_ref, act_ref, hb_ref, act0_ref, *, final_norm):
    nb, lb, d = x_ref.shape

    def normed(xr, shr, scr):
        h = _rmsnorm(xr[...], nw_ref[0]) * (1.0 + scr[0]) + shr[0]
        return h.reshape(nb * lb, d).astype(BF16)

    def act_chunk(hb, c):
        lo = c * FF_CHUNK
        g = _dot(hb, wgu_ref[0, :, lo:lo + FF_CHUNK])
        u = _dot(hb, wgu_ref[0, :, D_FF + lo:D_FF + lo + FF_CHUNK])
        return (_silu(g) * u).astype(BF16)

    @pl.when(jnp.logical_and(pl.program_id(0) == 0, pl.program_id(1) == 0))
    def _():
        hb0 = normed(x_ref, sh_ref, sc_ref)
        hb_ref[...] = hb0
        act0_ref[...] = act_chunk(hb0, 0)

    hb = hb_ref[...]
    act_ref[:, 0:FF_CHUNK] = act0_ref[...]
    n_chunks = D_FF // FF_CHUNK
    for c in range(1, n_chunks):
        act_ref[:, c * FF_CHUNK:(c + 1) * FF_CHUNK] = act_chunk(hb, c)
        if c == n_chunks // 2:
            hb_next = normed(xn_ref, shn_ref, scn_ref)
            hb_ref[...] = hb_next
            act0_ref[...] = act_chunk(hb_next, 0)
    y = _dot(act_ref[...], wd_ref[0])
    out = x_ref[...] + (0.5 * gt_ref[0]) * y.reshape(nb, lb, d)
    if final_norm:
        out = _rmsnorm(out, nf_ref[...])
    o_ref[...] = out


def _ffn_call(x, mod, row0, k_mod, layer, norm_w, w_gu, w_down, norm_final, *, nb, lb,
              final_norm):
    NB, LB, d = x.shape
    rows = nb * lb
    n_tiles = LB // lb
    last = (NB // nb) * n_tiles - 1

    def nxt(b, i):
        flat = jnp.minimum(b * n_tiles + i + 1, last)
        return flat // n_tiles, flat % n_tiles

    def mod_next(k):
        return pl.BlockSpec((1, nb, 1, d), lambda b, i: (layer, row0 // nb + nxt(b, i)[0], 0, k))

    return pl.pallas_call(
        functools.partial(_ffn_body, final_norm=final_norm),
        grid=(NB // nb, LB // lb),
        in_specs=[
            pl.BlockSpec((nb, lb, d), lambda b, i: (b, i, 0)),
            pl.BlockSpec((nb, lb, d), lambda b, i: nxt(b, i) + (0,)),
            _mod_spec(nb, row0, layer, k_mod),
            _mod_spec(nb, row0, layer, k_mod + 1),
            _mod_spec(nb, row0, layer, k_mod + 2),
            mod_next(k_mod), mod_next(k_mod + 1),
            _layer_spec(norm_w.shape, layer),
            _layer_spec(w_gu.shape, layer),
            _layer_spec(w_down.shape, layer),
            pl.BlockSpec((1, d), lambda b, i: (0, 0)),
        ],
        out_specs=pl.BlockSpec((nb, lb, d), lambda b, i: (b, i, 0)),
        out_shape=jax.ShapeDtypeStruct(x.shape, F32),
        scratch_shapes=[pltpu.VMEM((rows, D_FF), BF16), pltpu.VMEM((rows, d), BF16),
                        pltpu.VMEM((rows, FF_CHUNK), BF16)],
        compiler_params=pltpu.CompilerParams(
            dimension_semantics=("arbitrary", "arbitrary"),
            vmem_limit_bytes=VMEM_LIMIT),
        name="ffn",
    )(x, x, mod, mod, mod, mod, mod, norm_w, w_gu, w_down, norm_final.reshape(1, d))


def _causal_conv(ext_ref, cur, w_ref, new_buf_ref, lb, k_w, c0):
    cols = slice(c0, c0 + cur.shape[-1])
    ext_ref[:, HALO:HALO + lb, cols] = cur
    first = HALO - (k_w - 1)
    acc = None
    for k in range(k_w):
        tap = ext_ref[:, first + k:first + k + lb, cols] * w_ref[0, k:k + 1, cols]
        acc = tap if acc is None else acc + tap
    new_buf_ref[0, :, :, cols] = ext_ref[:, lb + first:lb + HALO, cols]
    ext_ref[:, 0:HALO, cols] = ext_ref[:, lb:lb + HALO, cols]
    return acc


def _mixer_body(x_ref, sh_ref, sc_ref, gt_ref, sin_ref, bxin_ref, bsin_ref,
                nw_ref, wzx_ref, wrest_ref, wdt_ref, cw_ref, cb_ref, dtb_ref, alog_ref,
                dexp_ref, snorm_ref, wos_ref, scw_ref, wosc_ref, wo_ref, *rest,
                nb, lb, chunk, nsub, n_alias):
    rest = rest[n_alias:]
    (o_ref, sout_ref, bxout_ref, bsout_ref,
     extx_ref, exts_ref, xbc_ref, y_ref, dtda_ref, side_ref) = rest
    rows = nb * lb
    sub_rows = rows // nsub
    seg = min(lb, chunk)
    nseg = chunk // seg
    seg_shift = seg.bit_length() - 1
    assert seg == 1 << seg_shift and sub_rows % chunk == 0 and rows % nsub == 0
    half = D_MODEL // 2
    single = nsub == 1
    t = pl.program_id(1)
    base = 0 if single else pl.multiple_of(lax.rem(t, nsub) * sub_rows, sub_rows)
    env = {}

    @pl.when(t < nsub)
    def _():
        sout_ref[0] = sin_ref[0]

    @pl.when(t == 0)
    def _():
        extx_ref[:, HALO - (SSD_CONV - 1):HALO, :] = bxin_ref[0]
        exts_ref[:, HALO - (SC_CONV - 1):HALO, :] = bsin_ref[0]

    def proj(lo, width):
        if lo >= C_DT:
            w_ref, lo = wdt_ref, lo - C_DT
        elif lo >= C_SCB:
            w_ref, lo = wrest_ref, lo - C_SCB
        else:
            w_ref = wzx_ref
        return _dot(env["hb"], w_ref[0, :, lo:lo + width])

    def put_side(k, hf, val):
        if single:
            env["side", k, hf] = val
        else:
            side_ref[:, k * D_MODEL + hf * half:k * D_MODEL + (hf + 1) * half] = val

    def get_side(k):
        if single:
            return jnp.concatenate([env["side", k, 0], env["side", k, 1]], axis=1)
        return side_ref[:, k * D_MODEL:(k + 1) * D_MODEL]

    def rows_of(name, col, r0):
        if single:
            return env[name][r0:r0 + chunk]
        return dtda_ref[pl.ds(base + r0, chunk), col * LANES:(col + 1) * LANES]

    def side_scc(hf):
        env["scc", hf] = proj(C_SCC + hf * half, half)

    def side_v(hf):
        env["v", hf] = env.pop(("scc", hf)) * proj(C_SCH + hf * half, half)

    def side_conv(_):
        v = jnp.concatenate([env.pop(("v", 0)), env.pop(("v", 1))], axis=1)
        env["u"] = _causal_conv(exts_ref, v.reshape(nb, lb, D_MODEL), scw_ref, bsout_ref, lb,
                                SC_CONV, 0).reshape(rows, D_MODEL)

    def side_su(hf):
        lo = hf * half
        env["su", hf] = (proj(C_SCB + lo, half) * env["u"][:, lo:lo + half]).astype(BF16)

    def side_ysc(hf):
        if hf == 0:
            env["su"] = jnp.concatenate([env.pop(("su", 0)), env.pop(("su", 1))], axis=1)
        env["ysc", hf] = _dot(env["su"], wosc_ref[0, :, hf * half:(hf + 1) * half])

    def side_gsc(hf):
        put_side(2, hf, _sigmoid(proj(C_GSC + hf * half, half)) * env.pop(("ysc", hf)))

    def side_gssd(hf):
        put_side(1, hf, _sigmoid(proj(C_GSSD + hf * half, half)))

    def side_z(hf):
        put_side(0, hf, _silu(proj(C_Z + hf * half, half)))

    queue = [functools.partial(f, hf) for f, hf in (
        (side_scc, 0), (side_v, 0), (side_scc, 1), (side_v, 1), (side_conv, 0),
        (side_su, 0), (side_su, 1), (side_ysc, 0), (side_ysc, 1), (side_gsc, 0), (side_gsc, 1),
        (side_gssd, 0), (side_gssd, 1), (side_z, 0), (side_z, 1))]

    def side_step():
        if queue:
            queue.pop(0)()

    ri = lax.broadcasted_iota(jnp.int32, (chunk, chunk), 0)
    ci = lax.broadcasted_iota(jnp.int32, (chunk, chunk), 1)
    same_seq = (ri >> seg_shift) == (ci >> seg_shift)
    causal = jnp.logical_and(same_seq, ci <= ri)
    causal_sel = causal.astype(F32).astype(BF16)
    seq_sel = same_seq.astype(F32).astype(BF16)
    eh = lax.broadcasted_iota(jnp.int32, (LANES, SSD_INNER), 0)
    ej = lax.broadcasted_iota(jnp.int32, (LANES, SSD_INNER), 1)
    head_sel = (eh == ej // SSD_HEAD_DIM).astype(F32).astype(BF16)
    lane_head = lax.broadcasted_iota(jnp.int32, (chunk, GROUP_WIDTH), 1) // SSD_HEAD_DIM
    bc0 = SSD_INNER
    cc0 = SSD_INNER + SSD_GROUPS * SSD_STATE
    n_chunks = sub_rows // chunk

    def normed_input():
        x = x_ref[...]
        h = _rmsnorm(x, nw_ref[0]) * (1.0 + sc_ref[0]) + sh_ref[0]
        env["hb"] = h.reshape(rows, D_MODEL).astype(BF16)

    def dt_proj():
        dt_all = _softplus(proj(C_DT, DT_PAD) + dtb_ref[0])
        da_all = dt_all * (-jnp.exp(alog_ref[0]))
        if single:
            env["dt"], env["da"] = dt_all, da_all
        else:
            dtda_ref[:, 0:LANES] = dt_all
            dtda_ref[:, LANES:2 * LANES] = da_all

    def conv_piece(c0):
        piece = proj(C_XBC + c0, GROUP_WIDTH).reshape(nb, lb, GROUP_WIDTH)
        piece = _causal_conv(extx_ref, piece, cw_ref, bxout_ref, lb, SSD_CONV, c0)
        piece = _silu(piece + cb_ref[0, :, c0:c0 + GROUP_WIDTH])
        xbc_ref[:, c0:c0 + GROUP_WIDTH] = piece.reshape(rows, GROUP_WIDTH)

    def scan_decays(c):
        r0 = c * chunk
        dt = rows_of("dt", 0, r0)
        da = rows_of("da", 1, r0)
        acum = _dot_exact_lhs(causal_sel, da)
        if nseg == 1:
            atot = acum[chunk - 1:chunk, :]
        else:
            atot = _dot_exact_lhs(seq_sel, da)
        ex = _dot_hi_mid_rhs(
            jnp.concatenate([dt * jnp.exp(atot - acum), jnp.exp(acum)], axis=0), head_sel)
        return dict(acum=acum, atot=atot, to_end=ex[0:chunk], e_a=ex[chunk:2 * chunk],
                    acum_t=acum.T, dt_t=dt.T)

    def scan_cb(c, st):
        rsl = pl.ds(base + c * chunk, chunk)
        st["bg"] = [xbc_ref[rsl, bc0 + g * SSD_STATE:bc0 + (g + 1) * SSD_STATE].astype(BF16)
                    for g in range(SSD_GROUPS)]
        st["cbm"] = [
            _dot_nt(xbc_ref[rsl, cc0 + g * SSD_STATE:cc0 + (g + 1) * SSD_STATE].astype(BF16),
                    st["bg"][g]) for g in range(SSD_GROUPS)]

    def scan_increments(c, st):
        rsl = pl.ds(base + c * chunk, chunk)
        xs = xbc_ref[rsl, 0:SSD_INNER]
        xd = xs * st["to_end"]
        st["xs_b"] = xs.astype(BF16)
        st["s_add"] = [[
            _dot_tn(xd[j * seg:(j + 1) * seg, g * GROUP_WIDTH:(g + 1) * GROUP_WIDTH].astype(BF16),
                    st["bg"][g][j * seg:(j + 1) * seg]) for j in range(nseg)]
            for g in range(SSD_GROUPS)]

    def scan_outputs(c, st, g):
        r0 = c * chunk
        rsl = pl.ds(base + r0, chunk)
        gl = g * GROUP_WIDTH
        acum, acum_t, dt_t, atot = st["acum"], st["acum_t"], st["dt_t"], st["atot"]
        cg = xbc_ref[rsl, cc0 + g * SSD_STATE:cc0 + (g + 1) * SSD_STATE]
        cbm = st["cbm"][g]
        ms = []
        for r in range(HEADS_PER_GROUP):
            hh = g * HEADS_PER_GROUP + r
            sgm = acum[:, hh:hh + 1] - acum_t[hh:hh + 1, :]
            dec = jnp.exp(jnp.where(causal, sgm, -jnp.inf))
            ms.append((cbm * dec * dt_t[hh:hh + 1, :]).astype(BF16))
        mg = jnp.concatenate(ms, axis=1)
        xg = st["xs_b"][:, gl:gl + GROUP_WIDTH]
        rhs = jnp.concatenate(
            [jnp.where(lane_head == r, xg, jnp.zeros_like(xg))
             for r in range(HEADS_PER_GROUP)], axis=0)
        y_diag = _dot(mg, rhs)

        for j in range(nseg):
            q0 = j * seg
            b_loc = (r0 + q0) // lb
            s_old = sout_ref[0, b_loc, gl:gl + GROUP_WIDTH, :]
            y_off = _dot_nt(cg[q0:q0 + seg].astype(BF16), s_old.astype(BF16))
            y_ref[pl.ds(base + r0 + q0, seg), gl:gl + GROUP_WIDTH] = (
                y_diag[q0:q0 + seg] + y_off * st["e_a"][q0:q0 + seg, gl:gl + GROUP_WIDTH])
            s_add = st["s_add"][g][j]
            q_last = q0 + seg - 1 if nseg > 1 else 0
            for r in range(HEADS_PER_GROUP):
                hh = g * HEADS_PER_GROUP + r
                keep = jnp.exp(atot[q_last:q_last + 1, hh:hh + 1])
                p0 = r * SSD_HEAD_DIM
                sout_ref[0, b_loc, gl + p0:gl + p0 + SSD_HEAD_DIM, :] = (
                    keep * s_old[p0:p0 + SSD_HEAD_DIM] + s_add[p0:p0 + SSD_HEAD_DIM])

    def before_scan():
        normed_input()
        dt_proj()
        for c0 in range(0, D_XBC, GROUP_WIDTH):
            conv_piece(c0)
            side_step()
        while queue:
            side_step()

    def scan():
        for c in range(n_chunks):
            st = scan_decays(c)
            scan_cb(c, st)
            scan_increments(c, st)
            for g in range(SSD_GROUPS):
                scan_outputs(c, st, g)

    def fused_step():
        normed_input()
        for c0 in range(0, D_XBC, GROUP_WIDTH):
            conv_piece(c0)
            side_step()
        dt_proj()
        sts = []
        for c in range(n_chunks):
            sts.append(scan_decays(c))
            scan_cb(c, sts[c])
            scan_increments(c, sts[c])
            side_step()
        for c in range(n_chunks):
            for g in range(SSD_GROUPS):
                scan_outputs(c, sts[c], g)
                side_step()
        after_scan()

    def after_scan():
        while queue:
            side_step()
        y = (y_ref[...] + xbc_ref[:, 0:SSD_INNER] * dexp_ref[0]) * get_side(0)
        parts = []
        for g in range(SSD_GROUPS):
            yg = y[:, g * GROUP_WIDTH:(g + 1) * GROUP_WIDTH]
            parts.append(yg * lax.rsqrt(jnp.mean(yg * yg, axis=-1, keepdims=True) + EPS))
        yn = jnp.concatenate(parts, axis=1) * snorm_ref[0]
        y_ssd = _dot(yn.astype(BF16), wos_ref[0])
        merged = get_side(1) * y_ssd + get_side(2)
        out = _dot(merged.astype(BF16), wo_ref[0]).reshape(nb, lb, D_MODEL)
        o_ref[...] = x_ref[...] + gt_ref[0] * out

    if single:
        fused_step()
    else:
        sub = lax.rem(t, nsub)
        pl.when(sub == 0)(before_scan)
        scan()
        pl.when(sub == nsub - 1)(after_scan)


def _mixer_call(x, mod, row0, layer, state_layer, s_in, bx_in, bs_in, lw, prev, *, nb, lb,
                chunk, nsub):
    NB, LB, d = x.shape
    rows = nb * lb
    sb = nb // nsub
    depth = lw["w_in_zx"].shape[0]
    assert nsub == 1 or LB == lb

    def per_tile(arr, lead):
        return pl.BlockSpec((1, nb) + arr.shape[2:], lambda b, t: (lead, b, 0, 0))

    def per_step(arr, lead):
        return pl.BlockSpec((1, sb) + arr.shape[2:],
                            lambda b, t: (lead, b * nsub + t % nsub, 0, 0))

    consts = [lw["norm_mix"], lw["w_in_zx"], lw["w_in_rest"], lw["w_in_dt"], lw["ssd_conv_w"],
              lw["ssd_conv_b"], lw["dt_bias"], lw["a_log"], lw["d_exp"], lw["ssd_norm"],
              lw["w_out_ssd"], lw["sc_conv_w"], lw["w_out_sc"], lw["w_o"]]
    state_shapes = [(depth,) + a.shape[1:] for a in (s_in, bx_in, bs_in)]
    n_in = 7 + len(consts)
    prev = list(prev) if prev is not None else []
    x_spec = pl.BlockSpec((nb, lb, d), lambda b, t: (b, t // nsub, 0))
    handover_rows = rows if nsub > 1 else SUBLANES
    return pl.pallas_call(
        functools.partial(_mixer_body, nb=nb, lb=lb, chunk=chunk, nsub=nsub, n_alias=len(prev)),
        grid=(NB // nb, (LB // lb) * nsub),
        in_specs=[
            x_spec,
            _mod_spec(nb, row0, layer, 3), _mod_spec(nb, row0, layer, 4),
            _mod_spec(nb, row0, layer, 5),
            per_step(s_in, state_layer), per_tile(bx_in, state_layer),
            per_tile(bs_in, state_layer),
        ] + [_layer_spec(w.shape, layer) for w in consts]
          + [pl.BlockSpec(memory_space=pl.ANY) for _ in prev],
        out_specs=[
            x_spec, per_step(s_in, layer), per_tile(bx_in, layer), per_tile(bs_in, layer),
        ],
        out_shape=[jax.ShapeDtypeStruct(x.shape, F32)]
                  + [jax.ShapeDtypeStruct(s, F32) for s in state_shapes],
        input_output_aliases={n_in + k: 1 + k for k in range(len(prev))},
        scratch_shapes=[
            pltpu.VMEM((nb, HALO + lb, D_XBC), F32),
            pltpu.VMEM((nb, HALO + lb, D_MODEL), F32),
            pltpu.VMEM((rows, D_XBC), F32),
            pltpu.VMEM((rows, SSD_INNER), F32),
            pltpu.VMEM((handover_rows, 2 * LANES), F32),
            pltpu.VMEM((handover_rows, 3 * D_MODEL), F32),
        ],
        compiler_params=pltpu.CompilerParams(
            dimension_semantics=("arbitrary", "arbitrary"),
            vmem_limit_bytes=VMEM_LIMIT),
        name="mixer",
    )(x, mod, mod, mod, s_in, bx_in, bs_in, *consts, *prev)


def _mixer_weights(w_in, norm_mix, ssd_conv_w, ssd_conv_b, ssd_dt_bias, ssd_a_log, ssd_d,
                   ssd_norm, w_out_ssd, sc_conv_w, w_out_sc, w_o):
    depth, d, _ = w_in.shape
    dt_lo = SSD_INNER + D_XBC
    dt_hi = dt_lo + SSD_HEADS
    pad = lambda v: jnp.pad(v, ((0, 0), (0, DT_PAD - SSD_HEADS))).reshape(depth, 1, DT_PAD)
    w_b = w_in.astype(BF16)
    return {
        "norm_mix": norm_mix.reshape(depth, 1, d),
        "w_in_zx": w_b[:, :, :dt_lo],
        "w_in_rest": w_b[:, :, dt_hi:],
        "w_in_dt": jnp.pad(w_b[:, :, dt_lo:dt_hi], ((0, 0), (0, 0), (0, DT_PAD - SSD_HEADS))),
        "ssd_conv_w": ssd_conv_w,
        "ssd_conv_b": ssd_conv_b.reshape(depth, 1, D_XBC),
        "dt_bias": pad(ssd_dt_bias),
        "a_log": pad(ssd_a_log),
        "d_exp": jnp.repeat(ssd_d, SSD_HEAD_DIM, axis=1).reshape(depth, 1, SSD_INNER),
        "ssd_norm": ssd_norm.reshape(depth, 1, SSD_INNER),
        "w_out_ssd": w_out_ssd.astype(BF16),
        "sc_conv_w": sc_conv_w,
        "w_out_sc": w_out_sc.astype(BF16),
        "w_o": w_o.astype(BF16),
    }


def kernel(x_prompt, x_sample, c_prompt, c_sample, state_ssm, state_conv_ssd, state_conv_short, w_ada, b_ada, norm_ffn1, norm_mix, norm_ffn2, ffn1_w_gu, ffn1_w_down, ffn2_w_gu, ffn2_w_down, w_in, ssd_conv_w, ssd_conv_b, ssd_dt_bias, ssd_a_log, ssd_d, ssd_norm, w_out_ssd, sc_conv_w, w_out_sc, w_o, norm_final):
    depth = w_ada.shape[0]
    bp, lp, d = x_prompt.shape
    bs, ls, _ = x_sample.shape
    hp = SSD_HEADS * SSD_HEAD_DIM

    mod = _mod_call(jnp.concatenate([c_sample, c_prompt], axis=0), w_ada, b_ada)
    row_s, row_p = 0, bs

    zeros_s = jnp.zeros((1, bp, hp, SSD_STATE), F32)
    zeros_bx = jnp.zeros((1, bp, SSD_CONV - 1, D_XBC), F32)
    zeros_bs = jnp.zeros((1, bp, SC_CONV - 1, D_MODEL), F32)
    state_s = state_ssm.reshape(depth, bs, hp, SSD_STATE)

    ffn_p = dict(nb=1, lb=min(FFN_ROWS, lp))
    ffn_s = dict(nb=min(FFN_ROWS // ls, bs), lb=ls)
    mix_p = dict(nb=1, lb=min(MIX_ROWS_PROMPT, lp), chunk=min(SSD_CHUNK, lp), nsub=1)
    nb_s = min(MIX_BATCH_SAMPLE, bs)
    nsub_s = min(MIX_SUBSTEPS_SAMPLE, nb_s)
    mix_s = dict(nb=nb_s, lb=ls, chunk=nb_s // nsub_s * ls, nsub=nsub_s)

    n1 = norm_ffn1.reshape(depth, 1, d)
    n2 = norm_ffn2.reshape(depth, 1, d)
    gu1, dn1 = ffn1_w_gu.astype(BF16), ffn1_w_down.astype(BF16)
    gu2, dn2 = ffn2_w_gu.astype(BF16), ffn2_w_down.astype(BF16)
    lw = _mixer_weights(w_in, norm_mix, ssd_conv_w, ssd_conv_b, ssd_dt_bias, ssd_a_log, ssd_d,
                        ssd_norm, w_out_ssd, sc_conv_w, w_out_sc, w_o)

    xp, xs = x_prompt, x_sample
    st_p = st_s = None
    for l in range(depth):
        last = l == depth - 1
        xp = _ffn_call(xp, mod, row_p, 0, l, n1, gu1, dn1, norm_final, final_norm=False, **ffn_p)
        xs = _ffn_call(xs, mod, row_s, 0, l, n1, gu1, dn1, norm_final, final_norm=False, **ffn_s)

        xp, *st_p = _mixer_call(xp, mod, row_p, l, 0, zeros_s, zeros_bx, zeros_bs, lw, st_p,
                                **mix_p)
        xs, *st_s = _mixer_call(xs, mod, row_s, l, l, state_s, state_conv_ssd, state_conv_short,
                                lw, st_s, **mix_s)

        xp = _ffn_call(xp, mod, row_p, 6, l, n2, gu2, dn2, norm_final, final_norm=last, **ffn_p)
        xs = _ffn_call(xs, mod, row_s, 6, l, n2, gu2, dn2, norm_final, final_norm=last, **ffn_s)

    shp = (SSD_HEADS, SSD_HEAD_DIM, SSD_STATE)
    return (xp, xs,
            st_p[0].reshape((depth, bp) + shp), st_p[1], st_p[2],
            st_s[0].reshape((depth, bs) + shp), st_s[1], st_s[2])
```

```python
import functools

import jax
import jax.numpy as jnp
from jax import lax
from jax.experimental import pallas as pl
from jax.experimental.pallas import tpu as pltpu

F32 = jnp.float32
BF16 = jnp.bfloat16

D_MODEL = 1024
SSD_HEADS = 16
SSD_HEAD_DIM = 64
SSD_GROUPS = 4
SSD_STATE = 128
HEADS_PER_GROUP = SSD_HEADS // SSD_GROUPS
GROUP_WIDTH = HEADS_PER_GROUP * SSD_HEAD_DIM
SSD_INNER = SSD_HEADS * SSD_HEAD_DIM
D_XBC = SSD_INNER + 2 * SSD_GROUPS * SSD_STATE
SSD_CONV = 4
SC_CONV = 3
D_FF = 2816
N_MOD = 9
EPS = 1e-6

LANES = 128
SUBLANES = 8
HALO = SUBLANES
DT_PAD = LANES

C_Z = 0
C_XBC = C_Z + SSD_INNER
C_SCB = C_XBC + D_XBC
C_SCC = C_SCB + D_MODEL
C_SCH = C_SCC + D_MODEL
C_GSSD = C_SCH + D_MODEL
C_GSC = C_GSSD + D_MODEL
C_DT = C_GSC + D_MODEL

FF_CHUNK = 256
FFN_ROWS = 512
MIX_ROWS_PROMPT = 256
MIX_BATCH_SAMPLE = 16
MIX_SUBSTEPS_SAMPLE = 4
SSD_CHUNK = 128
VMEM_LIMIT = 56 * 1024 * 1024


def _dot(a, b):
    return jnp.dot(a, b, preferred_element_type=F32)


def _dot_nt(a, b):
    return lax.dot_general(a, b, (((1,), (1,)), ((), ())), preferred_element_type=F32)


def _dot_tn(a, b):
    return lax.dot_general(a, b, (((0,), (0,)), ((), ())), preferred_element_type=F32)


def _dot_exact_lhs(sel, x):
    hi = x.astype(BF16)
    r1 = x - hi.astype(F32)
    mid = r1.astype(BF16)
    lo = (r1 - mid.astype(F32)).astype(BF16)
    return _dot(sel, hi) + _dot(sel, mid) + _dot(sel, lo)


def _silu(x):
    return x / (1.0 + jnp.exp(-x))


def _sigmoid(x):
    return 1.0 / (1.0 + jnp.exp(-x))


def _softplus(x):
    return jnp.maximum(x, 0.0) + jnp.log1p(jnp.exp(-jnp.abs(x)))


def _rmsnorm(x, g):
    r = lax.rsqrt(jnp.mean(x * x, axis=-1, keepdims=True) + EPS)
    return (x * r) * g


def _layer_spec(shape, layer):
    nd = len(shape) - 1
    return pl.BlockSpec((1,) + tuple(shape[1:]), lambda *_: (layer,) + (0,) * nd,
                        pipeline_mode=pl.Buffered(1))


def _mod_spec(nb, row0, layer, k):
    assert row0 % nb == 0
    return pl.BlockSpec((1, nb, 1, D_MODEL), lambda b, i: (layer, row0 // nb + b, 0, k))


def _mod_body(c_ref, w_ref, b_ref, o_ref):
    sc = _silu(c_ref[...]).astype(BF16)
    o_ref[0, :, 0, :] = _dot(sc, w_ref[0].astype(BF16)) + b_ref[0]


def _mod_call(c_all, w_ada, b_ada):
    depth, d, n = w_ada.shape
    m = c_all.shape[0]
    tn = D_MODEL
    return pl.pallas_call(
        _mod_body,
        grid=(depth, n // tn),
        in_specs=[
            pl.BlockSpec((m, d), lambda l, j: (0, 0)),
            pl.BlockSpec((1, d, tn), lambda l, j: (l, 0, j)),
            pl.BlockSpec((1, 1, tn), lambda l, j: (l, 0, j)),
        ],
        out_specs=pl.BlockSpec((1, m, 1, tn), lambda l, j: (l, 0, 0, j)),
        out_shape=jax.ShapeDtypeStruct((depth, m, 1, n), F32),
        compiler_params=pltpu.CompilerParams(
            dimension_semantics=("arbitrary", "arbitrary"),
            vmem_limit_bytes=VMEM_LIMIT),
        name="adaln_mod",
    )(c_all, w_ada, b_ada.reshape(depth, 1, n))


def _ffn_body(x_ref, xn_ref, sh_ref, sc_ref, gt_ref, shn_ref, scn_ref, nw_ref, wgu_ref, wd_ref,
              nf_ref, o_ref, act_ref, hb_ref, act0_ref, *, final_norm):
    nb, lb, d = x_ref.shape

    def normed(xr, shr, scr):
        h = _rmsnorm(xr[...], nw_ref[0]) * (1.0 + scr[0]) + shr[0]
        return h.reshape(nb * lb, d).astype(BF16)

    def act_chunk(hb, c):
        lo = c * FF_CHUNK
        g = _dot(hb, wgu_ref[0, :, lo:lo + FF_CHUNK])
        u = _dot(hb, wgu_ref[0, :, D_FF + lo:D_FF + lo + FF_CHUNK])
        return (_silu(g) * u).astype(BF16)

    @pl.when(jnp.logical_and(pl.program_id(0) == 0, pl.program_id(1) == 0))
    def _():
        hb0 = normed(x_ref, sh_ref, sc_ref)
        hb_ref[...] = hb0
        act0_ref[...] = act_chunk(hb0, 0)

    hb = hb_ref[...]
    act_ref[:, 0:FF_CHUNK] = act0_ref[...]
    n_chunks = D_FF // FF_CHUNK
    for c in range(1, n_chunks):
        act_ref[:, c * FF_CHUNK:(c + 1) * FF_CHUNK] = act_chunk(hb, c)
        if c == n_chunks // 2:
            hb_next = normed(xn_ref, shn_ref, scn_ref)
            hb_ref[...] = hb_next
            act0_ref[...] = act_chunk(hb_next, 0)
    y = _dot(act_ref[...], wd_ref[0])
    out = x_ref[...] + (0.5 * gt_ref[0]) * y.reshape(nb, lb, d)
    if final_norm:
        out = _rmsnorm(out, nf_ref[...])
    o_ref[...] = out


def _ffn_call(x, mod, row0, k_mod, layer, norm_w, w_gu, w_down, norm_final, *, nb, lb,
              final_norm):
    NB, LB, d = x.shape
    rows = nb * lb
    n_tiles = LB // lb
    last = (NB // nb) * n_tiles - 1

    def nxt(b, i):
        flat = jnp.minimum(b * n_tiles + i + 1, last)
        return flat // n_tiles, flat % n_tiles

    def mod_next(k):
        return pl.BlockSpec((1, nb, 1, d), lambda b, i: (layer, row0 // nb + nxt(b, i)[0], 0, k))

    return pl.pallas_call(
        functools.partial(_ffn_body, final_norm=final_norm),
        grid=(NB // nb, LB // lb),
        in_specs=[
            pl.BlockSpec((nb, lb, d), lambda b, i: (b, i, 0)),
            pl.BlockSpec((nb, lb, d), lambda b, i: nxt(b, i) + (0,)),
            _mod_spec(nb, row0, layer, k_mod),
            _mod_spec(nb, row0, layer, k_mod + 1),
            _mod_spec(nb, row0, layer, k_mod + 2),
            mod_next(k_mod), mod_next(k_mod + 1),
            _layer_spec(norm_w.shape, layer),
            _layer_spec(w_gu.shape, layer),
            _layer_spec(w_down.shape, layer),
            pl.BlockSpec((1, d), lambda b, i: (0, 0)),
        ],
        out_specs=pl.BlockSpec((nb, lb, d), lambda b, i: (b, i, 0)),
        out_shape=jax.ShapeDtypeStruct(x.shape, F32),
        scratch_shapes=[pltpu.VMEM((rows, D_FF), BF16), pltpu.VMEM((rows, d), BF16),
                        pltpu.VMEM((rows, FF_CHUNK), BF16)],
        compiler_params=pltpu.CompilerParams(
            dimension_semantics=("arbitrary", "arbitrary"),
            vmem_limit_bytes=VMEM_LIMIT),
        name="ffn",
    )(x, x, mod, mod, mod, mod, mod, norm_w, w_gu, w_down, norm_final.reshape(1, d))


def _causal_conv(ext_ref, cur, w_ref, new_buf_ref, lb, k_w, c0):
    cols = slice(c0, c0 + cur.shape[-1])
    ext_ref[:, HALO:HALO + lb, cols] = cur
    first = HALO - (k_w - 1)
    acc = None
    for k in range(k_w):
        tap = ext_ref[:, first + k:first + k + lb, cols] * w_ref[0, k:k + 1, cols]
        acc = tap if acc is None else acc + tap
    new_buf_ref[0, :, :, cols] = ext_ref[:, lb + first:lb + HALO, cols]
    ext_ref[:, 0:HALO, cols] = ext_ref[:, lb:lb + HALO, cols]
    return acc


def _mixer_body(x_ref, sh_ref, sc_ref, gt_ref, sin_ref, bxin_ref, bsin_ref,
                nw_ref, wzx_ref, wrest_ref, wdt_ref, cw_ref, cb_ref, dtb_ref, alog_ref,
                dexp_ref, snorm_ref, wos_ref, scw_ref, wosc_ref, wo_ref, *rest,
                nb, lb, chunk, nsub, n_alias):
    rest = rest[n_alias:]
    (o_ref, sout_ref, bxout_ref, bsout_ref,
     extx_ref, exts_ref, xbc_ref, y_ref, dtda_ref, side_ref) = rest
    rows = nb * lb
    sub_rows = rows // nsub
    seg = min(lb, chunk)
    nseg = chunk // seg
    seg_shift = seg.bit_length() - 1
    assert seg == 1 << seg_shift and sub_rows % chunk == 0 and rows % nsub == 0
    half = D_MODEL // 2
    single = nsub == 1
    t = pl.program_id(1)
    base = 0 if single else pl.multiple_of(lax.rem(t, nsub) * sub_rows, sub_rows)
    env = {}

    @pl.when(t < nsub)
    def _():
        sout_ref[0] = sin_ref[0]

    @pl.when(t == 0)
    def _():
        extx_ref[:, HALO - (SSD_CONV - 1):HALO, :] = bxin_ref[0]
        exts_ref[:, HALO - (SC_CONV - 1):HALO, :] = bsin_ref[0]

    def proj(lo, width):
        if lo >= C_DT:
            w_ref, lo = wdt_ref, lo - C_DT
        elif lo >= C_SCB:
            w_ref, lo = wrest_ref, lo - C_SCB
        else:
            w_ref = wzx_ref
        return _dot(env["hb"], w_ref[0, :, lo:lo + width])

    def put_side(k, hf, val):
        if single:
            env["side", k, hf] = val
        else:
            side_ref[:, k * D_MODEL + hf * half:k * D_MODEL + (hf + 1) * half] = val

    def get_side(k):
        if single:
            return jnp.concatenate([env["side", k, 0], env["side", k, 1]], axis=1)
        return side_ref[:, k * D_MODEL:(k + 1) * D_MODEL]

    def rows_of(name, col, r0):
        if single:
            return env[name][r0:r0 + chunk]
        return dtda_ref[pl.ds(base + r0, chunk), col * LANES:(col + 1) * LANES]

    def side_scc(hf):
        env["scc", hf] = proj(C_SCC + hf * half, half)

    def side_v(hf):
        env["v", hf] = env.pop(("scc", hf)) * proj(C_SCH + hf * half, half)

    def side_conv(_):
        v = jnp.concatenate([env.pop(("v", 0)), env.pop(("v", 1))], axis=1)
        env["u"] = _causal_conv(exts_ref, v.reshape(nb, lb, D_MODEL), scw_ref, bsout_ref, lb,
                                SC_CONV, 0).reshape(rows, D_MODEL)

    def side_su(hf):
        lo = hf * half
        env["su", hf] = (proj(C_SCB + lo, half) * env["u"][:, lo:lo + half]).astype(BF16)

    def side_ysc(hf):
        if hf == 0:
            env["su"] = jnp.concatenate([env.pop(("su", 0)), env.pop(("su", 1))], axis=1)
        env["ysc", hf] = _dot(env["su"], wosc_ref[0, :, hf * half:(hf + 1) * half])

    def side_gsc(hf):
        put_side(2, hf, _sigmoid(proj(C_GSC + hf * half, half)) * env.pop(("ysc", hf)))

    def side_gssd(hf):
        put_side(1, hf, _sigmoid(proj(C_GSSD + hf * half, half)))

    def side_z(hf):
        put_side(0, hf, _silu(proj(C_Z + hf * half, half)))

    queue = [functools.partial(f, hf) for f, hf in (
        (side_scc, 0), (side_v, 0), (side_scc, 1), (side_v, 1), (side_conv, 0),
        (side_su, 0), (side_su, 1), (side_ysc, 0), (side_ysc, 1), (side_gsc, 0), (side_gsc, 1),
        (side_gssd, 0), (side_gssd, 1), (side_z, 0), (side_z, 1))]

    def side_step():
        if queue:
            queue.pop(0)()

    ri = lax.broadcasted_iota(jnp.int32, (chunk, chunk), 0)
    ci = lax.broadcasted_iota(jnp.int32, (chunk, chunk), 1)
    same_seq = (ri >> seg_shift) == (ci >> seg_shift)
    causal = jnp.logical_and(same_seq, ci <= ri)
    causal_sel = causal.astype(F32).astype(BF16)
    seq_sel = same_seq.astype(F32).astype(BF16)
    eh = lax.broadcasted_iota(jnp.int32, (LANES, SSD_INNER), 0)
    ej = lax.broadcasted_iota(jnp.int32, (LANES, SSD_INNER), 1)
    head_sel = (eh == ej // SSD_HEAD_DIM).astype(F32).astype(BF16)
    lane_head = lax.broadcasted_iota(jnp.int32, (chunk, GROUP_WIDTH), 1) // SSD_HEAD_DIM
    bc0 = SSD_INNER
    cc0 = SSD_INNER + SSD_GROUPS * SSD_STATE
    n_chunks = sub_rows // chunk

    def normed_input():
        x = x_ref[...]
        h = _rmsnorm(x, nw_ref[0]) * (1.0 + sc_ref[0]) + sh_ref[0]
        env["hb"] = h.reshape(rows, D_MODEL).astype(BF16)

    def dt_proj():
        dt_all = _softplus(proj(C_DT, DT_PAD) + dtb_ref[0])
        da_all = dt_all * (-jnp.exp(alog_ref[0]))
        if single:
            env["dt"], env["da"] = dt_all, da_all
        else:
            dtda_ref[:, 0:LANES] = dt_all
            dtda_ref[:, LANES:2 * LANES] = da_all

    def conv_piece(c0):
        piece = proj(C_XBC + c0, GROUP_WIDTH).reshape(nb, lb, GROUP_WIDTH)
        piece = _causal_conv(extx_ref, piece, cw_ref, bxout_ref, lb, SSD_CONV, c0)
        piece = _silu(piece + cb_ref[0, :, c0:c0 + GROUP_WIDTH])
        xbc_ref[:, c0:c0 + GROUP_WIDTH] = piece.reshape(rows, GROUP_WIDTH)

    def scan_decays(c):
        r0 = c * chunk
        dt = rows_of("dt", 0, r0)
        da = rows_of("da", 1, r0)
        acum = _dot_exact_lhs(causal_sel, da)
        if nseg == 1:
            atot = acum[chunk - 1:chunk, :]
        else:
            atot = _dot_exact_lhs(seq_sel, da)
        ex = _dot(jnp.concatenate([dt * jnp.exp(atot - acum), jnp.exp(acum)],
                                  axis=0).astype(BF16), head_sel)
        return dict(acum=acum, atot=atot, to_end=ex[0:chunk], e_a=ex[chunk:2 * chunk],
                    acum_t=acum.T, dt_t=dt.T)

    def scan_cb(c, st):
        rsl = pl.ds(base + c * chunk, chunk)
        st["bg"] = [xbc_ref[rsl, bc0 + g * SSD_STATE:bc0 + (g + 1) * SSD_STATE].astype(BF16)
                    for g in range(SSD_GROUPS)]
        st["cbm"] = [
            _dot_nt(xbc_ref[rsl, cc0 + g * SSD_STATE:cc0 + (g + 1) * SSD_STATE].astype(BF16),
                    st["bg"][g]) for g in range(SSD_GROUPS)]

    def scan_increments(c, st):
        rsl = pl.ds(base + c * chunk, chunk)
        xs = xbc_ref[rsl, 0:SSD_INNER]
        xd = xs * st["to_end"]
        st["xs_b"] = xs.astype(BF16)
        st["s_add"] = [[
            _dot_tn(xd[j * seg:(j + 1) * seg, g * GROUP_WIDTH:(g + 1) * GROUP_WIDTH].astype(BF16),
                    st["bg"][g][j * seg:(j + 1) * seg]) for j in range(nseg)]
            for g in range(SSD_GROUPS)]

    def scan_outputs(c, st, g):
        r0 = c * chunk
        rsl = pl.ds(base + r0, chunk)
        gl = g * GROUP_WIDTH
        acum, acum_t, dt_t, atot = st["acum"], st["acum_t"], st["dt_t"], st["atot"]
        cg = xbc_ref[rsl, cc0 + g * SSD_STATE:cc0 + (g + 1) * SSD_STATE]
        cbm = st["cbm"][g]
        ms = []
        for r in range(HEADS_PER_GROUP):
            hh = g * HEADS_PER_GROUP + r
            sgm = acum[:, hh:hh + 1] - acum_t[hh:hh + 1, :]
            dec = jnp.exp(jnp.where(causal, sgm, -jnp.inf))
            ms.append((cbm * dec * dt_t[hh:hh + 1, :]).astype(BF16))
        mg = jnp.concatenate(ms, axis=1)
        xg = st["xs_b"][:, gl:gl + GROUP_WIDTH]
        rhs = jnp.concatenate(
            [jnp.where(lane_head == r, xg, jnp.zeros_like(xg))
             for r in range(HEADS_PER_GROUP)], axis=0)
        y_diag = _dot(mg, rhs)

        for j in range(nseg):
            q0 = j * seg
            b_loc = (r0 + q0) // lb
            s_old = sout_ref[0, b_loc, gl:gl + GROUP_WIDTH, :]
            y_off = _dot_nt(cg[q0:q0 + seg].astype(BF16), s_old.astype(BF16))
            y_ref[pl.ds(base + r0 + q0, seg), gl:gl + GROUP_WIDTH] = (
                y_diag[q0:q0 + seg] + y_off * st["e_a"][q0:q0 + seg, gl:gl + GROUP_WIDTH])
            s_add = st["s_add"][g][j]
            q_last = q0 + seg - 1 if nseg > 1 else 0
            for r in range(HEADS_PER_GROUP):
                hh = g * HEADS_PER_GROUP + r
                keep = jnp.exp(atot[q_last:q_last + 1, hh:hh + 1])
                p0 = r * SSD_HEAD_DIM
                sout_ref[0, b_loc, gl + p0:gl + p0 + SSD_HEAD_DIM, :] = (
                    keep * s_old[p0:p0 + SSD_HEAD_DIM] + s_add[p0:p0 + SSD_HEAD_DIM])

    def before_scan():
        normed_input()
        dt_proj()
        for c0 in range(0, D_XBC, GROUP_WIDTH):
            conv_piece(c0)
            side_step()
        while queue:
            side_step()

    def scan():
        for c in range(n_chunks):
            st = scan_decays(c)
            scan_cb(c, st)
            scan_increments(c, st)
            for g in range(SSD_GROUPS):
                scan_outputs(c, st, g)

    def fused_step():
        normed_input()
        for c0 in range(0, D_XBC, GROUP_WIDTH):
            conv_piece(c0)
            side_step()
        dt_proj()
        sts = []
        for c in range(n_chunks):
            sts.append(scan_decays(c))
            scan_cb(c, sts[c])
            scan_increments(c, sts[c])
            side_step()
        for c in range(n_chunks):
            for g in range(SSD_GROUPS):
                scan_outputs(c, sts[c], g)
                side_step()
        after_scan()

    def after_scan():
        while queue:
            side_step()
        y = (y_ref[...] + xbc_ref[:, 0:SSD_INNER] * dexp_ref[0]) * get_side(0)
        parts = []
        for g in range(SSD_GROUPS):
            yg = y[:, g * GROUP_WIDTH:(g + 1) * GROUP_WIDTH]
            parts.append(yg * lax.rsqrt(jnp.mean(yg * yg, axis=-1, keepdims=True) + EPS))
        yn = jnp.concatenate(parts, axis=1) * snorm_ref[0]
        y_ssd = _dot(yn.astype(BF16), wos_ref[0])
        merged = get_side(1) * y_ssd + get_side(2)
        out = _dot(merged.astype(BF16), wo_ref[0]).reshape(nb, lb, D_MODEL)
        o_ref[...] = x_ref[...] + gt_ref[0] * out

    if single:
        fused_step()
    else:
        sub = lax.rem(t, nsub)
        pl.when(sub == 0)(before_scan)
        scan()
        pl.when(sub == nsub - 1)(after_scan)


def _mixer_call(x, mod, row0, layer, state_layer, s_in, bx_in, bs_in, lw, prev, *, nb, lb,
                chunk, nsub):
    NB, LB, d = x.shape
    rows = nb * lb
    sb = nb // nsub
    depth = lw["w_in_zx"].shape[0]
    assert nsub == 1 or LB == lb

    def per_tile(arr, lead):
        return pl.BlockSpec((1, nb) + arr.shape[2:], lambda b, t: (lead, b, 0, 0))

    def per_step(arr, lead):
        return pl.BlockSpec((1, sb) + arr.shape[2:],
                            lambda b, t: (lead, b * nsub + t % nsub, 0, 0))

    consts = [lw["norm_mix"], lw["w_in_zx"], lw["w_in_rest"], lw["w_in_dt"], lw["ssd_conv_w"],
              lw["ssd_conv_b"], lw["dt_bias"], lw["a_log"], lw["d_exp"], lw["ssd_norm"],
              lw["w_out_ssd"], lw["sc_conv_w"], lw["w_out_sc"], lw["w_o"]]
    state_shapes = [(depth,) + a.shape[1:] for a in (s_in, bx_in, bs_in)]
    n_in = 7 + len(consts)
    prev = list(prev) if prev is not None else []
    x_spec = pl.BlockSpec((nb, lb, d), lambda b, t: (b, t // nsub, 0))
    handover_rows = rows if nsub > 1 else SUBLANES
    return pl.pallas_call(
        functools.partial(_mixer_body, nb=nb, lb=lb, chunk=chunk, nsub=nsub, n_alias=len(prev)),
        grid=(NB // nb, (LB // lb) * nsub),
        in_specs=[
            x_spec,
            _mod_spec(nb, row0, layer, 3), _mod_spec(nb, row0, layer, 4),
            _mod_spec(nb, row0, layer, 5),
            per_step(s_in, state_layer), per_tile(bx_in, state_layer),
            per_tile(bs_in, state_layer),
        ] + [_layer_spec(w.shape, layer) for w in consts]
          + [pl.BlockSpec(memory_space=pl.ANY) for _ in prev],
        out_specs=[
            x_spec, per_step(s_in, layer), per_tile(bx_in, layer), per_tile(bs_in, layer),
        ],
        out_shape=[jax.ShapeDtypeStruct(x.shape, F32)]
                  + [jax.ShapeDtypeStruct(s, F32) for s in state_shapes],
        input_output_aliases={n_in + k: 1 + k for k in range(len(prev))},
        scratch_shapes=[
            pltpu.VMEM((nb, HALO + lb, D_XBC), F32),
            pltpu.VMEM((nb, HALO + lb, D_MODEL), F32),
            pltpu.VMEM((rows, D_XBC), F32),
            pltpu.VMEM((rows, SSD_INNER), F32),
            pltpu.VMEM((handover_rows, 2 * LANES), F32),
            pltpu.VMEM((handover_rows, 3 * D_MODEL), F32),
        ],
        compiler_params=pltpu.CompilerParams(
            dimension_semantics=("arbitrary", "arbitrary"),
            vmem_limit_bytes=VMEM_LIMIT),
        name="mixer",
    )(x, mod, mod, mod, s_in, bx_in, bs_in, *consts, *prev)


def _mixer_weights(w_in, norm_mix, ssd_conv_w, ssd_conv_b, ssd_dt_bias, ssd_a_log, ssd_d,
                   ssd_norm, w_out_ssd, sc_conv_w, w_out_sc, w_o):
    depth, d, _ = w_in.shape
    dt_lo = SSD_INNER + D_XBC
    dt_hi = dt_lo + SSD_HEADS
    pad = lambda v: jnp.pad(v, ((0, 0), (0, DT_PAD - SSD_HEADS))).reshape(depth, 1, DT_PAD)
    w_b = lax.optimization_barrier(w_in.astype(BF16))
    return {
        "norm_mix": norm_mix.reshape(depth, 1, d),
        "w_in_zx": w_b[:, :, :dt_lo],
        "w_in_rest": w_b[:, :, dt_hi:],
        "w_in_dt": jnp.pad(w_b[:, :, dt_lo:dt_hi], ((0, 0), (0, 0), (0, DT_PAD - SSD_HEADS))),
        "ssd_conv_w": ssd_conv_w,
        "ssd_conv_b": ssd_conv_b.reshape(depth, 1, D_XBC),
        "dt_bias": pad(ssd_dt_bias),
        "a_log": pad(ssd_a_log),
        "d_exp": jnp.repeat(ssd_d, SSD_HEAD_DIM, axis=1).reshape(depth, 1, SSD_INNER),
        "ssd_norm": ssd_norm.reshape(depth, 1, SSD_INNER),
        "w_out_ssd": w_out_ssd.astype(BF16),
        "sc_conv_w": sc_conv_w,
        "w_out_sc": w_out_sc.astype(BF16),
        "w_o": w_o.astype(BF16),
    }


def kernel(x_prompt, x_sample, c_prompt, c_sample, state_ssm, state_conv_ssd, state_conv_short, w_ada, b_ada, norm_ffn1, norm_mix, norm_ffn2, ffn1_w_gu, ffn1_w_down, ffn2_w_gu, ffn2_w_down, w_in, ssd_conv_w, ssd_conv_b, ssd_dt_bias, ssd_a_log, ssd_d, ssd_norm, w_out_ssd, sc_conv_w, w_out_sc, w_o, norm_final):
    depth = w_ada.shape[0]
    bp, lp, d = x_prompt.shape
    bs, ls, _ = x_sample.shape
    hp = SSD_HEADS * SSD_HEAD_DIM

    mod = _mod_call(jnp.concatenate([c_sample, c_prompt], axis=0), w_ada, b_ada)
    row_s, row_p = 0, bs

    zeros_s = jnp.zeros((1, bp, hp, SSD_STATE), F32)
    zeros_bx = jnp.zeros((1, bp, SSD_CONV - 1, D_XBC), F32)
    zeros_bs = jnp.zeros((1, bp, SC_CONV - 1, D_MODEL), F32)
    state_s = state_ssm.reshape(depth, bs, hp, SSD_STATE)

    ffn_p = dict(nb=1, lb=min(FFN_ROWS, lp))
    ffn_s = dict(nb=min(FFN_ROWS // ls, bs), lb=ls)
    mix_p = dict(nb=1, lb=min(MIX_ROWS_PROMPT, lp), chunk=min(SSD_CHUNK, lp), nsub=1)
    nb_s = min(MIX_BATCH_SAMPLE, bs)
    nsub_s = min(MIX_SUBSTEPS_SAMPLE, nb_s)
    mix_s = dict(nb=nb_s, lb=ls, chunk=nb_s // nsub_s * ls, nsub=nsub_s)

    n1 = norm_ffn1.reshape(depth, 1, d)
    n2 = norm_ffn2.reshape(depth, 1, d)
    gu1, dn1 = ffn1_w_gu.astype(BF16), ffn1_w_down.astype(BF16)
    gu2, dn2 = ffn2_w_gu.astype(BF16), ffn2_w_down.astype(BF16)
    lw = _mixer_weights(w_in, norm_mix, ssd_conv_w, ssd_conv_b, ssd_dt_bias, ssd_a_log, ssd_d,
                        ssd_norm, w_out_ssd, sc_conv_w, w_out_sc, w_o)

    xp, xs = x_prompt, x_sample
    st_p = st_s = None
    for l in range(depth):
        last = l == depth - 1
        xp = _ffn_call(xp, mod, row_p, 0, l, n1, gu1, dn1, norm_final, final_norm=False, **ffn_p)
        xs = _ffn_call(xs, mod, row_s, 0, l, n1, gu1, dn1, norm_final, final_norm=False, **ffn_s)

        xp, *st_p = _mixer_call(xp, mod, row_p, l, 0, zeros_s, zeros_bx, zeros_bs, lw, st_p,
                                **mix_p)
        xs, *st_s = _mixer_call(xs, mod, row_s, l, l, state_s, state_conv_ssd, state_conv_short,
                                lw, st_s, **mix_s)

        xp = _ffn_call(xp, mod, row_p, 6, l, n2, gu2, dn2, norm_final, final_norm=last, **ffn_p)
        xs = _ffn_call(xs, mod, row_s, 6, l, n2, gu2, dn2, norm_final, final_norm=last, **ffn_s)

    shp = (SSD_HEADS, SSD_HEAD_DIM, SSD_STATE)
    return (xp, xs,
            st_p[0].reshape((depth, bp) + shp), st_p[1], st_p[2],
            st_s[0].reshape((depth, bs) + shp), st_s[1], st_s[2])
```

```python
import functools

import jax
import jax.numpy as jnp
from jax import lax
from jax.experimental import pallas as pl
from jax.experimental.pallas import tpu as pltpu

F32 = jnp.float32
BF16 = jnp.bfloat16

D_MODEL = 1024
SSD_HEADS = 16
SSD_HEAD_DIM = 64
SSD_GROUPS = 4
SSD_STATE = 128
HEADS_PER_GROUP = SSD_HEADS // SSD_GROUPS
GROUP_WIDTH = HEADS_PER_GROUP * SSD_HEAD_DIM
SSD_INNER = SSD_HEADS * SSD_HEAD_DIM
D_XBC = SSD_INNER + 2 * SSD_GROUPS * SSD_STATE
SSD_CONV = 4
SC_CONV = 3
D_FF = 2816
N_MOD = 9
EPS = 1e-6

LANES = 128
SUBLANES = 8
HALO = SUBLANES
DT_PAD = LANES

C_Z = 0
C_XBC = C_Z + SSD_INNER
C_SCB = C_XBC + D_XBC
C_SCC = C_SCB + D_MODEL
C_SCH = C_SCC + D_MODEL
C_GSSD = C_SCH + D_MODEL
C_GSC = C_GSSD + D_MODEL
C_DT = C_GSC + D_MODEL

FF_CHUNK = 256
FFN_ROWS = 512
MIX_ROWS_PROMPT = 256
MIX_BATCH_SAMPLE = 16
MIX_SUBSTEPS_SAMPLE = 4
SSD_CHUNK = 128
VMEM_LIMIT = 56 * 1024 * 1024


def _dot(a, b):
    return jnp.dot(a, b, preferred_element_type=F32)


def _dot_nt(a, b):
    return lax.dot_general(a, b, (((1,), (1,)), ((), ())), preferred_element_type=F32)


def _dot_tn(a, b):
    return lax.dot_general(a, b, (((0,), (0,)), ((), ())), preferred_element_type=F32)


def _dot_exact_lhs(sel, x):
    hi = x.astype(BF16)
    r1 = x - hi.astype(F32)
    mid = r1.astype(BF16)
    lo = (r1 - mid.astype(F32)).astype(BF16)
    return _dot(sel, hi) + _dot(sel, mid) + _dot(sel, lo)


def _silu(x):
    return x / (1.0 + jnp.exp(-x))


def _sigmoid(x):
    return 1.0 / (1.0 + jnp.exp(-x))


def _softplus(x):
    return jnp.maximum(x, 0.0) + jnp.log1p(jnp.exp(-jnp.abs(x)))


def _rmsnorm(x, g):
    r = lax.rsqrt(jnp.mean(x * x, axis=-1, keepdims=True) + EPS)
    return (x * r) * g


def _layer_spec(shape, layer):
    nd = len(shape) - 1
    return pl.BlockSpec((1,) + tuple(shape[1:]), lambda *_: (layer,) + (0,) * nd,
                        pipeline_mode=pl.Buffered(1))


def _mod_spec(nb, row0, layer, k):
    assert row0 % nb == 0
    return pl.BlockSpec((1, nb, 1, D_MODEL), lambda b, i: (layer, row0 // nb + b, 0, k))


def _mod_body(c_ref, w_ref, b_ref, o_ref):
    sc = _silu(c_ref[...]).astype(BF16)
    o_ref[0, :, 0, :] = _dot(sc, w_ref[0].astype(BF16)) + b_ref[0]


def _mod_call(c_all, w_ada, b_ada):
    depth, d, n = w_ada.shape
    m = c_all.shape[0]
    tn = D_MODEL
    return pl.pallas_call(
        _mod_body,
        grid=(depth, n // tn),
        in_specs=[
            pl.BlockSpec((m, d), lambda l, j: (0, 0)),
            pl.BlockSpec((1, d, tn), lambda l, j: (l, 0, j)),
            pl.BlockSpec((1, 1, tn), lambda l, j: (l, 0, j)),
        ],
        out_specs=pl.BlockSpec((1, m, 1, tn), lambda l, j: (l, 0, 0, j)),
        out_shape=jax.ShapeDtypeStruct((depth, m, 1, n), F32),
        compiler_params=pltpu.CompilerParams(
            dimension_semantics=("arbitrary", "arbitrary"),
            vmem_limit_bytes=VMEM_LIMIT),
        name="adaln_mod",
    )(c_all, w_ada, b_ada.reshape(depth, 1, n))


def _ffn_body(x_ref, xn_ref, sh_ref, sc_ref, gt_ref, shn_ref, scn_ref, nw_ref, wgu_ref, wd_ref,
              nf_ref, o_ref, act_ref, hb_ref, act0_ref, *, final_norm):
    nb, lb, d = x_ref.shape

    def normed(xr, shr, scr):
        h = _rmsnorm(xr[...], nw_ref[0]) * (1.0 + scr[0]) + shr[0]
        return h.reshape(nb * lb, d).astype(BF16)

    def act_chunk(hb, c):
        lo = c * FF_CHUNK
        g = _dot(hb, wgu_ref[0, :, lo:lo + FF_CHUNK])
        u = _dot(hb, wgu_ref[0, :, D_FF + lo:D_FF + lo + FF_CHUNK])
        return (_silu(g) * u).astype(BF16)

    @pl.when(jnp.logical_and(pl.program_id(0) == 0, pl.program_id(1) == 0))
    def _():
        hb0 = normed(x_ref, sh_ref, sc_ref)
        hb_ref[...] = hb0
        act0_ref[...] = act_chunk(hb0, 0)

    hb = hb_ref[...]
    act_ref[:, 0:FF_CHUNK] = act0_ref[...]
    n_chunks = D_FF // FF_CHUNK
    for c in range(1, n_chunks):
        act_ref[:, c * FF_CHUNK:(c + 1) * FF_CHUNK] = act_chunk(hb, c)
        if c == n_chunks // 2:
            hb_next = normed(xn_ref, shn_ref, scn_ref)
            hb_ref[...] = hb_next
            act0_ref[...] = act_chunk(hb_next, 0)
    y = _dot(act_ref[...], wd_ref[0])
    out = x_ref[...] + (0.5 * gt_ref[0]) * y.reshape(nb, lb, d)
    if final_norm:
        out = _rmsnorm(out, nf_ref[...])
    o_ref[...] = out


def _ffn_call(x, mod, row0, k_mod, layer, norm_w, w_gu, w_down, norm_final, *, nb, lb,
              final_norm):
    NB, LB, d = x.shape
    rows = nb * lb
    n_tiles = LB // lb
    last = (NB // nb) * n_tiles - 1

    def nxt(b, i):
        flat = jnp.minimum(b * n_tiles + i + 1, last)
        return flat // n_tiles, flat % n_tiles

    def mod_next(k):
        return pl.BlockSpec((1, nb, 1, d), lambda b, i: (layer, row0 // nb + nxt(b, i)[0], 0, k))

    return pl.pallas_call(
        functools.partial(_ffn_body, final_norm=final_norm),
        grid=(NB // nb, LB // lb),
        in_specs=[
            pl.BlockSpec((nb, lb, d), lambda b, i: (b, i, 0)),
            pl.BlockSpec((nb, lb, d), lambda b, i: nxt(b, i) + (0,)),
            _mod_spec(nb, row0, layer, k_mod),
            _mod_spec(nb, row0, layer, k_mod + 1),
            _mod_spec(nb, row0, layer, k_mod + 2),
            mod_next(k_mod), mod_next(k_mod + 1),
            _layer_spec(norm_w.shape, layer),
            _layer_spec(w_gu.shape, layer),
            _layer_spec(w_down.shape, layer),
            pl.BlockSpec((1, d), lambda b, i: (0, 0)),
        ],
        out_specs=pl.BlockSpec((nb, lb, d), lambda b, i: (b, i, 0)),
        out_shape=jax.ShapeDtypeStruct(x.shape, F32),
        scratch_shapes=[pltpu.VMEM((rows, D_FF), BF16), pltpu.VMEM((rows, d), BF16),
                        pltpu.VMEM((rows, FF_CHUNK), BF16)],
        compiler_params=pltpu.CompilerParams(
            dimension_semantics=("arbitrary", "arbitrary"),
            vmem_limit_bytes=VMEM_LIMIT),
        name="ffn",
    )(x, x, mod, mod, mod, mod, mod, norm_w, w_gu, w_down, norm_final.reshape(1, d))


def _causal_conv(ext_ref, cur, w_ref, new_buf_ref, lb, k_w, c0):
    cols = slice(c0, c0 + cur.shape[-1])
    ext_ref[:, HALO:HALO + lb, cols] = cur
    first = HALO - (k_w - 1)
    acc = None
    for k in range(k_w):
        tap = ext_ref[:, first + k:first + k + lb, cols] * w_ref[0, k:k + 1, cols]
        acc = tap if acc is None else acc + tap
    new_buf_ref[0, :, :, cols] = ext_ref[:, lb + first:lb + HALO, cols]
    ext_ref[:, 0:HALO, cols] = ext_ref[:, lb:lb + HALO, cols]
    return acc


def _mixer_body(x_ref, sh_ref, sc_ref, gt_ref, sin_ref, bxin_ref, bsin_ref,
                nw_ref, wzx_ref, wrest_ref, wdt_ref, cw_ref, cb_ref, dtb_ref, alog_ref,
                dexp_ref, snorm_ref, wos_ref, scw_ref, wosc_ref, wo_ref, *rest,
                nb, lb, chunk, nsub, n_alias):
    rest = rest[n_alias:]
    (o_ref, sout_ref, bxout_ref, bsout_ref,
     extx_ref, exts_ref, xbc_ref, y_ref, dtda_ref, side_ref) = rest
    rows = nb * lb
    sub_rows = rows // nsub
    seg = min(lb, chunk)
    nseg = chunk // seg
    seg_shift = seg.bit_length() - 1
    assert seg == 1 << seg_shift and sub_rows % chunk == 0 and rows % nsub == 0
    half = D_MODEL // 2
    single = nsub == 1
    t = pl.program_id(1)
    base = 0 if single else pl.multiple_of(lax.rem(t, nsub) * sub_rows, sub_rows)
    env = {}

    @pl.when(t < nsub)
    def _():
        sout_ref[0] = sin_ref[0]

    @pl.when(t == 0)
    def _():
        extx_ref[:, HALO - (SSD_CONV - 1):HALO, :] = bxin_ref[0]
        exts_ref[:, HALO - (SC_CONV - 1):HALO, :] = bsin_ref[0]

    def proj(lo, width):
        if lo >= C_DT:
            w_ref, lo = wdt_ref, lo - C_DT
        elif lo >= C_SCB:
            w_ref, lo = wrest_ref, lo - C_SCB
        else:
            w_ref = wzx_ref
        return _dot(env["hb"], w_ref[0, :, lo:lo + width])

    def put_side(k, hf, val):
        if single:
            env["side", k, hf] = val
        else:
            side_ref[:, k * D_MODEL + hf * half:k * D_MODEL + (hf + 1) * half] = val

    def get_side(k):
        if single:
            return jnp.concatenate([env["side", k, 0], env["side", k, 1]], axis=1)
        return side_ref[:, k * D_MODEL:(k + 1) * D_MODEL]

    def rows_of(name, col, r0):
        if single:
            return env[name][r0:r0 + chunk]
        return dtda_ref[pl.ds(base + r0, chunk), col * LANES:(col + 1) * LANES]

    def side_scc(hf):
        env["scc", hf] = proj(C_SCC + hf * half, half)

    def side_v(hf):
        env["v", hf] = env.pop(("scc", hf)) * proj(C_SCH + hf * half, half)

    def side_conv(_):
        v = jnp.concatenate([env.pop(("v", 0)), env.pop(("v", 1))], axis=1)
        env["u"] = _causal_conv(exts_ref, v.reshape(nb, lb, D_MODEL), scw_ref, bsout_ref, lb,
                                SC_CONV, 0).reshape(rows, D_MODEL)

    def side_su(hf):
        lo = hf * half
        env["su", hf] = (proj(C_SCB + lo, half) * env["u"][:, lo:lo + half]).astype(BF16)

    def side_ysc(hf):
        if hf == 0:
            env["su"] = jnp.concatenate([env.pop(("su", 0)), env.pop(("su", 1))], axis=1)
        env["ysc", hf] = _dot(env["su"], wosc_ref[0, :, hf * half:(hf + 1) * half])

    def side_gsc(hf):
        put_side(2, hf, _sigmoid(proj(C_GSC + hf * half, half)) * env.pop(("ysc", hf)))

    def side_gssd(hf):
        put_side(1, hf, _sigmoid(proj(C_GSSD + hf * half, half)))

    def side_z(hf):
        put_side(0, hf, _silu(proj(C_Z + hf * half, half)))

    queue = [functools.partial(f, hf) for f, hf in (
        (side_scc, 0), (side_v, 0), (side_scc, 1), (side_v, 1), (side_conv, 0),
        (side_su, 0), (side_su, 1), (side_ysc, 0), (side_ysc, 1), (side_gsc, 0), (side_gsc, 1),
        (side_gssd, 0), (side_gssd, 1), (side_z, 0), (side_z, 1))]

    def side_step():
        if queue:
            queue.pop(0)()

    ri = lax.broadcasted_iota(jnp.int32, (chunk, chunk), 0)
    ci = lax.broadcasted_iota(jnp.int32, (chunk, chunk), 1)
    same_seq = (ri >> seg_shift) == (ci >> seg_shift)
    causal = jnp.logical_and(same_seq, ci <= ri)
    causal_sel = causal.astype(F32).astype(BF16)
    seq_sel = same_seq.astype(F32).astype(BF16)
    eh = lax.broadcasted_iota(jnp.int32, (LANES, SSD_INNER), 0)
    ej = lax.broadcasted_iota(jnp.int32, (LANES, SSD_INNER), 1)
    head_sel = (eh == ej // SSD_HEAD_DIM).astype(F32).astype(BF16)
    lane_head = lax.broadcasted_iota(jnp.int32, (chunk, GROUP_WIDTH), 1) // SSD_HEAD_DIM
    bc0 = SSD_INNER
    cc0 = SSD_INNER + SSD_GROUPS * SSD_STATE
    n_chunks = sub_rows // chunk

    def normed_input():
        x = x_ref[...]
        h = _rmsnorm(x, nw_ref[0]) * (1.0 + sc_ref[0]) + sh_ref[0]
        env["hb"] = h.reshape(rows, D_MODEL).astype(BF16)

    def dt_proj():
        dt_all = _softplus(proj(C_DT, DT_PAD) + dtb_ref[0])
        da_all = dt_all * (-jnp.exp(alog_ref[0]))
        if single:
            env["dt"], env["da"] = dt_all, da_all
        else:
            dtda_ref[:, 0:LANES] = dt_all
            dtda_ref[:, LANES:2 * LANES] = da_all

    def conv_piece(c0):
        piece = proj(C_XBC + c0, GROUP_WIDTH).reshape(nb, lb, GROUP_WIDTH)
        piece = _causal_conv(extx_ref, piece, cw_ref, bxout_ref, lb, SSD_CONV, c0)
        piece = _silu(piece + cb_ref[0, :, c0:c0 + GROUP_WIDTH])
        xbc_ref[:, c0:c0 + GROUP_WIDTH] = piece.reshape(rows, GROUP_WIDTH)

    def scan_decays(c):
        r0 = c * chunk
        dt = rows_of("dt", 0, r0)
        da = rows_of("da", 1, r0)
        acum = _dot_exact_lhs(causal_sel, da)
        if nseg == 1:
            atot = acum[chunk - 1:chunk, :]
        else:
            atot = _dot_exact_lhs(seq_sel, da)
        ex = _dot(jnp.concatenate([dt * jnp.exp(atot - acum), jnp.exp(acum)],
                                  axis=0).astype(BF16), head_sel)
        return dict(acum=acum, atot=atot, to_end=ex[0:chunk], e_a=ex[chunk:2 * chunk],
                    acum_t=acum.T, dt_t=dt.T)

    def scan_cb(c, st):
        rsl = pl.ds(base + c * chunk, chunk)
        st["bg"] = [xbc_ref[rsl, bc0 + g * SSD_STATE:bc0 + (g + 1) * SSD_STATE].astype(BF16)
                    for g in range(SSD_GROUPS)]
        st["cbm"] = [
            _dot_nt(xbc_ref[rsl, cc0 + g * SSD_STATE:cc0 + (g + 1) * SSD_STATE].astype(BF16),
                    st["bg"][g]) for g in range(SSD_GROUPS)]

    def scan_increments(c, st):
        rsl = pl.ds(base + c * chunk, chunk)
        xs = xbc_ref[rsl, 0:SSD_INNER]
        xd = xs * st["to_end"]
        st["xs_b"] = xs.astype(BF16)
        st["s_add"] = [[
            _dot_tn(xd[j * seg:(j + 1) * seg, g * GROUP_WIDTH:(g + 1) * GROUP_WIDTH].astype(BF16),
                    st["bg"][g][j * seg:(j + 1) * seg]) for j in range(nseg)]
            for g in range(SSD_GROUPS)]

    def scan_outputs(c, st, g):
        r0 = c * chunk
        rsl = pl.ds(base + r0, chunk)
        gl = g * GROUP_WIDTH
        acum, acum_t, dt_t, atot = st["acum"], st["acum_t"], st["dt_t"], st["atot"]
        cg = xbc_ref[rsl, cc0 + g * SSD_STATE:cc0 + (g + 1) * SSD_STATE]
        cbm = st["cbm"][g]
        ms = []
        for r in range(HEADS_PER_GROUP):
            hh = g * HEADS_PER_GROUP + r
            sgm = acum[:, hh:hh + 1] - acum_t[hh:hh + 1, :]
            dec = jnp.exp(jnp.where(causal, sgm, -jnp.inf))
            ms.append((cbm * dec * dt_t[hh:hh + 1, :]).astype(BF16))
        mg = jnp.concatenate(ms, axis=1)
        xg = st["xs_b"][:, gl:gl + GROUP_WIDTH]
        rhs = jnp.concatenate(
            [jnp.where(lane_head == r, xg, jnp.zeros_like(xg))
             for r in range(HEADS_PER_GROUP)], axis=0)
        y_diag = _dot(mg, rhs)

        for j in range(nseg):
            q0 = j * seg
            b_loc = (r0 + q0) // lb
            s_old = sout_ref[0, b_loc, gl:gl + GROUP_WIDTH, :]
            y_off = _dot_nt(cg[q0:q0 + seg].astype(BF16), s_old.astype(BF16))
            y_ref[pl.ds(base + r0 + q0, seg), gl:gl + GROUP_WIDTH] = (
                y_diag[q0:q0 + seg] + y_off * st["e_a"][q0:q0 + seg, gl:gl + GROUP_WIDTH])
            s_add = st["s_add"][g][j]
            q_last = q0 + seg - 1 if nseg > 1 else 0
            for r in range(HEADS_PER_GROUP):
                hh = g * HEADS_PER_GROUP + r
                keep = jnp.exp(atot[q_last:q_last + 1, hh:hh + 1])
                p0 = r * SSD_HEAD_DIM
                sout_ref[0, b_loc, gl + p0:gl + p0 + SSD_HEAD_DIM, :] = (
                    keep * s_old[p0:p0 + SSD_HEAD_DIM] + s_add[p0:p0 + SSD_HEAD_DIM])

    def before_scan():
        normed_input()
        dt_proj()
        for c0 in range(0, D_XBC, GROUP_WIDTH):
            conv_piece(c0)
            side_step()
        while queue:
            side_step()

    def scan():
        for c in range(n_chunks):
            st = scan_decays(c)
            scan_cb(c, st)
            scan_increments(c, st)
            for g in range(SSD_GROUPS):
                scan_outputs(c, st, g)

    def fused_step():
        normed_input()
        for c0 in range(0, D_XBC, GROUP_WIDTH):
            conv_piece(c0)
            side_step()
        dt_proj()
        sts = []
        for c in range(n_chunks):
            sts.append(scan_decays(c))
            scan_cb(c, sts[c])
            scan_increments(c, sts[c])
            side_step()
        for c in range(n_chunks):
            for g in range(SSD_GROUPS):
                scan_outputs(c, sts[c], g)
                side_step()
        after_scan()

    def after_scan():
        while queue:
            side_step()
        y = (y_ref[...] + xbc_ref[:, 0:SSD_INNER] * dexp_ref[0]) * get_side(0)
        parts = []
        for g in range(SSD_GROUPS):
            yg = y[:, g * GROUP_WIDTH:(g + 1) * GROUP_WIDTH]
            parts.append(yg * lax.rsqrt(jnp.mean(yg * yg, axis=-1, keepdims=True) + EPS))
        yn = jnp.concatenate(parts, axis=1) * snorm_ref[0]
        y_ssd = _dot(yn.astype(BF16), wos_ref[0])
        merged = get_side(1) * y_ssd + get_side(2)
        out = _dot(merged.astype(BF16), wo_ref[0]).reshape(nb, lb, D_MODEL)
        o_ref[...] = x_ref[...] + gt_ref[0] * out

    if single:
        fused_step()
    else:
        sub = lax.rem(t, nsub)
        pl.when(sub == 0)(before_scan)
        scan()
        pl.when(sub == nsub - 1)(after_scan)


def _mixer_call(x, mod, row0, layer, state_layer, s_in, bx_in, bs_in, lw, prev, *, nb, lb,
                chunk, nsub):
    NB, LB, d = x.shape
    rows = nb * lb
    sb = nb // nsub
    depth = lw["w_in_zx"].shape[0]
    assert nsub == 1 or LB == lb

    def per_tile(arr, lead):
        return pl.BlockSpec((1, nb) + arr.shape[2:], lambda b, t: (lead, b, 0, 0))

    def per_step(arr, lead):
        return pl.BlockSpec((1, sb) + arr.shape[2:],
                            lambda b, t: (lead, b * nsub + t % nsub, 0, 0))

    consts = [lw["norm_mix"], lw["w_in_zx"], lw["w_in_rest"], lw["w_in_dt"], lw["ssd_conv_w"],
              lw["ssd_conv_b"], lw["dt_bias"], lw["a_log"], lw["d_exp"], lw["ssd_norm"],
              lw["w_out_ssd"], lw["sc_conv_w"], lw["w_out_sc"], lw["w_o"]]
    state_shapes = [(depth,) + a.shape[1:] for a in (s_in, bx_in, bs_in)]
    n_in = 7 + len(consts)
    prev = list(prev) if prev is not None else []
    x_spec = pl.BlockSpec((nb, lb, d), lambda b, t: (b, t // nsub, 0))
    handover_rows = rows if nsub > 1 else SUBLANES
    return pl.pallas_call(
        functools.partial(_mixer_body, nb=nb, lb=lb, chunk=chunk, nsub=nsub, n_alias=len(prev)),
        grid=(NB // nb, (LB // lb) * nsub),
        in_specs=[
            x_spec,
            _mod_spec(nb, row0, layer, 3), _mod_spec(nb, row0, layer, 4),
            _mod_spec(nb, row0, layer, 5),
            per_step(s_in, state_layer), per_tile(bx_in, state_layer),
            per_tile(bs_in, state_layer),
        ] + [_layer_spec(w.shape, layer) for w in consts]
          + [pl.BlockSpec(memory_space=pl.ANY) for _ in prev],
        out_specs=[
            x_spec, per_step(s_in, layer), per_tile(bx_in, layer), per_tile(bs_in, layer),
        ],
        out_shape=[jax.ShapeDtypeStruct(x.shape, F32)]
                  + [jax.ShapeDtypeStruct(s, F32) for s in state_shapes],
        input_output_aliases={n_in + k: 1 + k for k in range(len(prev))},
        scratch_shapes=[
            pltpu.VMEM((nb, HALO + lb, D_XBC), F32),
            pltpu.VMEM((nb, HALO + lb, D_MODEL), F32),
            pltpu.VMEM((rows, D_XBC), F32),
            pltpu.VMEM((rows, SSD_INNER), F32),
            pltpu.VMEM((handover_rows, 2 * LANES), F32),
            pltpu.VMEM((handover_rows, 3 * D_MODEL), F32),
        ],
        compiler_params=pltpu.CompilerParams(
            dimension_semantics=("arbitrary", "arbitrary"),
            vmem_limit_bytes=VMEM_LIMIT),
        name="mixer",
    )(x, mod, mod, mod, s_in, bx_in, bs_in, *consts, *prev)


def _mixer_weights(w_in, norm_mix, ssd_conv_w, ssd_conv_b, ssd_dt_bias, ssd_a_log, ssd_d,
                   ssd_norm, w_out_ssd, sc_conv_w, w_out_sc, w_o):
    depth, d, _ = w_in.shape
    dt_lo = SSD_INNER + D_XBC
    dt_hi = dt_lo + SSD_HEADS
    pad = lambda v: jnp.pad(v, ((0, 0), (0, DT_PAD - SSD_HEADS))).reshape(depth, 1, DT_PAD)
    w_b = w_in.astype(BF16)
    return {
        "norm_mix": norm_mix.reshape(depth, 1, d),
        "w_in_zx": w_b[:, :, :dt_lo],
        "w_in_rest": w_b[:, :, dt_hi:],
        "w_in_dt": jnp.pad(w_b[:, :, dt_lo:dt_hi], ((0, 0), (0, 0), (0, DT_PAD - SSD_HEADS))),
        "ssd_conv_w": ssd_conv_w,
        "ssd_conv_b": ssd_conv_b.reshape(depth, 1, D_XBC),
        "dt_bias": pad(ssd_dt_bias),
        "a_log": pad(ssd_a_log),
        "d_exp": jnp.repeat(ssd_d, SSD_HEAD_DIM, axis=1).reshape(depth, 1, SSD_INNER),
        "ssd_norm": ssd_norm.reshape(depth, 1, SSD_INNER),
        "w_out_ssd": w_out_ssd.astype(BF16),
        "sc_conv_w": sc_conv_w,
        "w_out_sc": w_out_sc.astype(BF16),
        "w_o": w_o.astype(BF16),
    }


def kernel(x_prompt, x_sample, c_prompt, c_sample, state_ssm, state_conv_ssd, state_conv_short, w_ada, b_ada, norm_ffn1, norm_mix, norm_ffn2, ffn1_w_gu, ffn1_w_down, ffn2_w_gu, ffn2_w_down, w_in, ssd_conv_w, ssd_conv_b, ssd_dt_bias, ssd_a_log, ssd_d, ssd_norm, w_out_ssd, sc_conv_w, w_out_sc, w_o, norm_final):
    depth = w_ada.shape[0]
    bp, lp, d = x_prompt.shape
    bs, ls, _ = x_sample.shape
    hp = SSD_HEADS * SSD_HEAD_DIM

    mod = _mod_call(jnp.concatenate([c_sample, c_prompt], axis=0), w_ada, b_ada)
    row_s, row_p = 0, bs

    zeros_s = jnp.zeros((1, bp, hp, SSD_STATE), F32)
    zeros_bx = jnp.zeros((1, bp, SSD_CONV - 1, D_XBC), F32)
    zeros_bs = jnp.zeros((1, bp, SC_CONV - 1, D_MODEL), F32)
    state_s = state_ssm.reshape(depth, bs, hp, SSD_STATE)

    ffn_p = dict(nb=1, lb=min(2 * FFN_ROWS, lp))
    ffn_s = dict(nb=min(FFN_ROWS // ls, bs), lb=ls)
    mix_p = dict(nb=1, lb=min(MIX_ROWS_PROMPT, lp), chunk=min(SSD_CHUNK, lp), nsub=1)
    nb_s = min(MIX_BATCH_SAMPLE, bs)
    nsub_s = min(MIX_SUBSTEPS_SAMPLE, nb_s)
    mix_s = dict(nb=nb_s, lb=ls, chunk=nb_s // nsub_s * ls, nsub=nsub_s)

    n1 = norm_ffn1.reshape(depth, 1, d)
    n2 = norm_ffn2.reshape(depth, 1, d)
    gu1, dn1 = ffn1_w_gu.astype(BF16), ffn1_w_down.astype(BF16)
    gu2, dn2 = ffn2_w_gu.astype(BF16), ffn2_w_down.astype(BF16)
    lw = _mixer_weights(w_in, norm_mix, ssd_conv_w, ssd_conv_b, ssd_dt_bias, ssd_a_log, ssd_d,
                        ssd_norm, w_out_ssd, sc_conv_w, w_out_sc, w_o)

    xp, xs = x_prompt, x_sample
    st_p = st_s = None
    for l in range(depth):
        last = l == depth - 1
        xp = _ffn_call(xp, mod, row_p, 0, l, n1, gu1, dn1, norm_final, final_norm=False, **ffn_p)
        xs = _ffn_call(xs, mod, row_s, 0, l, n1, gu1, dn1, norm_final, final_norm=False, **ffn_s)

        xp, *st_p = _mixer_call(xp, mod, row_p, l, 0, zeros_s, zeros_bx, zeros_bs, lw, st_p,
                                **mix_p)
        xs, *st_s = _mixer_call(xs, mod, row_s, l, l, state_s, state_conv_ssd, state_conv_short,
                                lw, st_s, **mix_s)

        xp = _ffn_call(xp, mod, row_p, 6, l, n2, gu2, dn2, norm_final, final_norm=last, **ffn_p)
        xs = _ffn_call(xs, mod, row_s, 6, l, n2, gu2, dn2, norm_final, final_norm=last, **ffn_s)

    shp = (SSD_HEADS, SSD_HEAD_DIM, SSD_STATE)
    return (xp, xs,
            st_p[0].reshape((depth, bp) + shp), st_p[1], st_p[2],
            st_s[0].reshape((depth, bs) + shp), st_s[1], st_s[2])
```

```python
import functools

import jax
import jax.numpy as jnp
from jax import lax
from jax.experimental import pallas as pl
from jax.experimental.pallas import tpu as pltpu

F32 = jnp.float32
BF16 = jnp.bfloat16

D_MODEL = 1024
SSD_HEADS = 16
SSD_HEAD_DIM = 64
SSD_GROUPS = 4
SSD_STATE = 128
HEADS_PER_GROUP = SSD_HEADS // SSD_GROUPS
GROUP_WIDTH = HEADS_PER_GROUP * SSD_HEAD_DIM
SSD_INNER = SSD_HEADS * SSD_HEAD_DIM
D_XBC = SSD_INNER + 2 * SSD_GROUPS * SSD_STATE
SSD_CONV = 4
SC_CONV = 3
D_FF = 2816
N_MOD = 9
EPS = 1e-6

LANES = 128
SUBLANES = 8
HALO = SUBLANES
DT_PAD = LANES

C_Z = 0
C_XBC = C_Z + SSD_INNER
C_SCB = C_XBC + D_XBC
C_SCC = C_SCB + D_MODEL
C_SCH = C_SCC + D_MODEL
C_GSSD = C_SCH + D_MODEL
C_GSC = C_GSSD + D_MODEL
C_DT = C_GSC + D_MODEL

MOD_COLS = 2304
FF_CHUNK = 256
FFN_ROWS = 512
MIX_ROWS_PROMPT = 256
MIX_BATCH_SAMPLE = 16
MIX_SUBSTEPS_SAMPLE = 4
SSD_CHUNK = 128
VMEM_LIMIT = 56 * 1024 * 1024


def _dot(a, b):
    return jnp.dot(a, b, preferred_element_type=F32)


def _dot_nt(a, b):
    return lax.dot_general(a, b, (((1,), (1,)), ((), ())), preferred_element_type=F32)


def _dot_tn(a, b):
    return lax.dot_general(a, b, (((0,), (0,)), ((), ())), preferred_element_type=F32)


def _dot_exact_lhs(sel, x):
    hi = x.astype(BF16)
    r1 = x - hi.astype(F32)
    mid = r1.astype(BF16)
    lo = (r1 - mid.astype(F32)).astype(BF16)
    return _dot(sel, hi) + _dot(sel, mid) + _dot(sel, lo)


def _silu(x):
    return x / (1.0 + jnp.exp(-x))


def _sigmoid(x):
    return 1.0 / (1.0 + jnp.exp(-x))


def _softplus(x):
    return jnp.maximum(x, 0.0) + jnp.log1p(jnp.exp(-jnp.abs(x)))


def _rmsnorm(x, g):
    r = lax.rsqrt(jnp.mean(x * x, axis=-1, keepdims=True) + EPS)
    return (x * r) * g


def _layer_spec(shape, layer):
    nd = len(shape) - 1
    return pl.BlockSpec((1,) + tuple(shape[1:]), lambda *_: (layer,) + (0,) * nd,
                        pipeline_mode=pl.Buffered(1))


def _mod_spec(nb, row0, layer, k):
    assert row0 % nb == 0
    return pl.BlockSpec((1, nb, 1, D_MODEL), lambda b, i: (layer, row0 // nb + b, 0, k))


def _mod_body(c_ref, w_ref, b_ref, o_ref):
    sc = _silu(c_ref[...]).astype(BF16)
    o_ref[0, :, 0, :] = _dot(sc, w_ref[0].astype(BF16)) + b_ref[0]


def _mod_call(c_all, w_ada, b_ada):
    depth, d, n = w_ada.shape
    m = c_all.shape[0]
    tn = MOD_COLS
    return pl.pallas_call(
        _mod_body,
        grid=(depth, n // tn),
        in_specs=[
            pl.BlockSpec((m, d), lambda l, j: (0, 0)),
            pl.BlockSpec((1, d, tn), lambda l, j: (l, 0, j)),
            pl.BlockSpec((1, 1, tn), lambda l, j: (l, 0, j)),
        ],
        out_specs=pl.BlockSpec((1, m, 1, tn), lambda l, j: (l, 0, 0, j)),
        out_shape=jax.ShapeDtypeStruct((depth, m, 1, n), F32),
        compiler_params=pltpu.CompilerParams(
            dimension_semantics=("arbitrary", "arbitrary"),
            vmem_limit_bytes=VMEM_LIMIT),
        name="adaln_mod",
    )(c_all, w_ada, b_ada.reshape(depth, 1, n))


def _ffn_body(x_ref, xn_ref, sh_ref, sc_ref, gt_ref, shn_ref, scn_ref, nw_ref, wgu_ref, wd_ref,
              nf_ref, o_ref, act_ref, hb_ref, act0_ref, *, final_norm):
    nb, lb, d = x_ref.shape

    def normed(xr, shr, scr):
        h = _rmsnorm(xr[...], nw_ref[0]) * (1.0 + scr[0]) + shr[0]
        return h.reshape(nb * lb, d).astype(BF16)

    def act_chunk(hb, c):
        lo = c * FF_CHUNK
        g = _dot(hb, wgu_ref[0, :, lo:lo + FF_CHUNK])
        u = _dot(hb, wgu_ref[0, :, D_FF + lo:D_FF + lo + FF_CHUNK])
        return (_silu(g) * u).astype(BF16)

    @pl.when(jnp.logical_and(pl.program_id(0) == 0, pl.program_id(1) == 0))
    def _():
        hb0 = normed(x_ref, sh_ref, sc_ref)
        hb_ref[...] = hb0
        act0_ref[...] = act_chunk(hb0, 0)

    hb = hb_ref[...]
    act_ref[:, 0:FF_CHUNK] = act0_ref[...]
    n_chunks = D_FF // FF_CHUNK
    for c in range(1, n_chunks):
        act_ref[:, c * FF_CHUNK:(c + 1) * FF_CHUNK] = act_chunk(hb, c)
        if c == n_chunks // 2:
            hb_next = normed(xn_ref, shn_ref, scn_ref)
            hb_ref[...] = hb_next
            act0_ref[...] = act_chunk(hb_next, 0)
    y = _dot(act_ref[...], wd_ref[0])
    out = x_ref[...] + (0.5 * gt_ref[0]) * y.reshape(nb, lb, d)
    if final_norm:
        out = _rmsnorm(out, nf_ref[...])
    o_ref[...] = out


def _ffn_call(x, mod, row0, k_mod, layer, norm_w, w_gu, w_down, norm_final, *, nb, lb,
              final_norm):
    NB, LB, d = x.shape
    rows = nb * lb
    n_tiles = LB // lb
    last = (NB // nb) * n_tiles - 1

    def nxt(b, i):
        flat = jnp.minimum(b * n_tiles + i + 1, last)
        return flat // n_tiles, flat % n_tiles

    def mod_next(k):
        return pl.BlockSpec((1, nb, 1, d), lambda b, i: (layer, row0 // nb + nxt(b, i)[0], 0, k))

    return pl.pallas_call(
        functools.partial(_ffn_body, final_norm=final_norm),
        grid=(NB // nb, LB // lb),
        in_specs=[
            pl.BlockSpec((nb, lb, d), lambda b, i: (b, i, 0)),
            pl.BlockSpec((nb, lb, d), lambda b, i: nxt(b, i) + (0,)),
            _mod_spec(nb, row0, layer, k_mod),
            _mod_spec(nb, row0, layer, k_mod + 1),
            _mod_spec(nb, row0, layer, k_mod + 2),
            mod_next(k_mod), mod_next(k_mod + 1),
            _layer_spec(norm_w.shape, layer),
            _layer_spec(w_gu.shape, layer),
            _layer_spec(w_down.shape, layer),
            pl.BlockSpec((1, d), lambda b, i: (0, 0)),
        ],
        out_specs=pl.BlockSpec((nb, lb, d), lambda b, i: (b, i, 0)),
        out_shape=jax.ShapeDtypeStruct(x.shape, F32),
        scratch_shapes=[pltpu.VMEM((rows, D_FF), BF16), pltpu.VMEM((rows, d), BF16),
                        pltpu.VMEM((rows, FF_CHUNK), BF16)],
        compiler_params=pltpu.CompilerParams(
            dimension_semantics=("arbitrary", "arbitrary"),
            vmem_limit_bytes=VMEM_LIMIT),
        name="ffn",
    )(x, x, mod, mod, mod, mod, mod, norm_w, w_gu, w_down, norm_final.reshape(1, d))


def _causal_conv(ext_ref, cur, w_ref, new_buf_ref, lb, k_w, c0):
    cols = slice(c0, c0 + cur.shape[-1])
    ext_ref[:, HALO:HALO + lb, cols] = cur
    first = HALO - (k_w - 1)
    acc = None
    for k in range(k_w):
        tap = ext_ref[:, first + k:first + k + lb, cols] * w_ref[0, k:k + 1, cols]
        acc = tap if acc is None else acc + tap
    new_buf_ref[0, :, :, cols] = ext_ref[:, lb + first:lb + HALO, cols]
    ext_ref[:, 0:HALO, cols] = ext_ref[:, lb:lb + HALO, cols]
    return acc


def _mixer_body(x_ref, sh_ref, sc_ref, gt_ref, sin_ref, bxin_ref, bsin_ref,
                nw_ref, wzx_ref, wrest_ref, wdt_ref, cw_ref, cb_ref, dtb_ref, alog_ref,
                dexp_ref, snorm_ref, wos_ref, scw_ref, wosc_ref, wo_ref, *rest,
                nb, lb, chunk, nsub, n_alias):
    rest = rest[n_alias:]
    (o_ref, sout_ref, bxout_ref, bsout_ref,
     extx_ref, exts_ref, xbc_ref, y_ref, dtda_ref, side_ref) = rest
    rows = nb * lb
    sub_rows = rows // nsub
    seg = min(lb, chunk)
    nseg = chunk // seg
    seg_shift = seg.bit_length() - 1
    assert seg == 1 << seg_shift and sub_rows % chunk == 0 and rows % nsub == 0
    half = D_MODEL // 2
    single = nsub == 1
    t = pl.program_id(1)
    base = 0 if single else pl.multiple_of(lax.rem(t, nsub) * sub_rows, sub_rows)
    env = {}

    @pl.when(t < nsub)
    def _():
        sout_ref[0] = sin_ref[0]

    @pl.when(t == 0)
    def _():
        extx_ref[:, HALO - (SSD_CONV - 1):HALO, :] = bxin_ref[0]
        exts_ref[:, HALO - (SC_CONV - 1):HALO, :] = bsin_ref[0]

    def proj(lo, width):
        if lo >= C_DT:
            w_ref, lo = wdt_ref, lo - C_DT
        elif lo >= C_SCB:
            w_ref, lo = wrest_ref, lo - C_SCB
        else:
            w_ref = wzx_ref
        return _dot(env["hb"], w_ref[0, :, lo:lo + width])

    def put_side(k, hf, val):
        if single:
            env["side", k, hf] = val
        else:
            side_ref[:, k * D_MODEL + hf * half:k * D_MODEL + (hf + 1) * half] = val

    def get_side(k):
        if single:
            return jnp.concatenate([env["side", k, 0], env["side", k, 1]], axis=1)
        return side_ref[:, k * D_MODEL:(k + 1) * D_MODEL]

    def rows_of(name, col, r0):
        if single:
            return env[name][r0:r0 + chunk]
        return dtda_ref[pl.ds(base + r0, chunk), col * LANES:(col + 1) * LANES]

    def side_scc(hf):
        env["scc", hf] = proj(C_SCC + hf * half, half)

    def side_v(hf):
        env["v", hf] = env.pop(("scc", hf)) * proj(C_SCH + hf * half, half)

    def side_conv(_):
        v = jnp.concatenate([env.pop(("v", 0)), env.pop(("v", 1))], axis=1)
        env["u"] = _causal_conv(exts_ref, v.reshape(nb, lb, D_MODEL), scw_ref, bsout_ref, lb,
                                SC_CONV, 0).reshape(rows, D_MODEL)

    def side_su(hf):
        lo = hf * half
        env["su", hf] = (proj(C_SCB + lo, half) * env["u"][:, lo:lo + half]).astype(BF16)

    def side_ysc(hf):
        if hf == 0:
            env["su"] = jnp.concatenate([env.pop(("su", 0)), env.pop(("su", 1))], axis=1)
        env["ysc", hf] = _dot(env["su"], wosc_ref[0, :, hf * half:(hf + 1) * half])

    def side_gsc(hf):
        put_side(2, hf, _sigmoid(proj(C_GSC + hf * half, half)) * env.pop(("ysc", hf)))

    def side_gssd(hf):
        put_side(1, hf, _sigmoid(proj(C_GSSD + hf * half, half)))

    def side_z(hf):
        put_side(0, hf, _silu(proj(C_Z + hf * half, half)))

    queue = [functools.partial(f, hf) for f, hf in (
        (side_scc, 0), (side_v, 0), (side_scc, 1), (side_v, 1), (side_conv, 0),
        (side_su, 0), (side_su, 1), (side_ysc, 0), (side_ysc, 1), (side_gsc, 0), (side_gsc, 1),
        (side_gssd, 0), (side_gssd, 1), (side_z, 0), (side_z, 1))]

    def side_step():
        if queue:
            queue.pop(0)()

    ri = lax.broadcasted_iota(jnp.int32, (chunk, chunk), 0)
    ci = lax.broadcasted_iota(jnp.int32, (chunk, chunk), 1)
    same_seq = (ri >> seg_shift) == (ci >> seg_shift)
    causal = jnp.logical_and(same_seq, ci <= ri)
    causal_sel = causal.astype(F32).astype(BF16)
    seq_sel = same_seq.astype(F32).astype(BF16)
    eh = lax.broadcasted_iota(jnp.int32, (LANES, SSD_INNER), 0)
    ej = lax.broadcasted_iota(jnp.int32, (LANES, SSD_INNER), 1)
    head_sel = (eh == ej // SSD_HEAD_DIM).astype(F32).astype(BF16)
    lane_head = lax.broadcasted_iota(jnp.int32, (chunk, GROUP_WIDTH), 1) // SSD_HEAD_DIM
    bc0 = SSD_INNER
    cc0 = SSD_INNER + SSD_GROUPS * SSD_STATE
    n_chunks = sub_rows // chunk

    def normed_input():
        x = x_ref[...]
        h = _rmsnorm(x, nw_ref[0]) * (1.0 + sc_ref[0]) + sh_ref[0]
        env["hb"] = h.reshape(rows, D_MODEL).astype(BF16)

    def dt_proj():
        dt_all = _softplus(proj(C_DT, DT_PAD) + dtb_ref[0])
        da_all = dt_all * (-jnp.exp(alog_ref[0]))
        if single:
            env["dt"], env["da"] = dt_all, da_all
        else:
            dtda_ref[:, 0:LANES] = dt_all
            dtda_ref[:, LANES:2 * LANES] = da_all

    def conv_piece(c0):
        piece = proj(C_XBC + c0, GROUP_WIDTH).reshape(nb, lb, GROUP_WIDTH)
        piece = _causal_conv(extx_ref, piece, cw_ref, bxout_ref, lb, SSD_CONV, c0)
        piece = _silu(piece + cb_ref[0, :, c0:c0 + GROUP_WIDTH])
        xbc_ref[:, c0:c0 + GROUP_WIDTH] = piece.reshape(rows, GROUP_WIDTH)

    def scan_decays(c):
        r0 = c * chunk
        dt = rows_of("dt", 0, r0)
        da = rows_of("da", 1, r0)
        acum = _dot_exact_lhs(causal_sel, da)
        if nseg == 1:
            atot = acum[chunk - 1:chunk, :]
        else:
            atot = _dot_exact_lhs(seq_sel, da)
        ex = _dot(jnp.concatenate([dt * jnp.exp(atot - acum), jnp.exp(acum)],
                                  axis=0).astype(BF16), head_sel)
        return dict(acum=acum, atot=atot, to_end=ex[0:chunk], e_a=ex[chunk:2 * chunk],
                    acum_t=acum.T, dt_t=dt.T)

    def scan_cb(c, st):
        rsl = pl.ds(base + c * chunk, chunk)
        st["bg"] = [xbc_ref[rsl, bc0 + g * SSD_STATE:bc0 + (g + 1) * SSD_STATE].astype(BF16)
                    for g in range(SSD_GROUPS)]
        st["cbm"] = [
            _dot_nt(xbc_ref[rsl, cc0 + g * SSD_STATE:cc0 + (g + 1) * SSD_STATE].astype(BF16),
                    st["bg"][g]) for g in range(SSD_GROUPS)]

    def scan_increments(c, st):
        rsl = pl.ds(base + c * chunk, chunk)
        xs = xbc_ref[rsl, 0:SSD_INNER]
        xd = xs * st["to_end"]
        st["xs_b"] = xs.astype(BF16)
        st["s_add"] = [[
            _dot_tn(xd[j * seg:(j + 1) * seg, g * GROUP_WIDTH:(g + 1) * GROUP_WIDTH].astype(BF16),
                    st["bg"][g][j * seg:(j + 1) * seg]) for j in range(nseg)]
            for g in range(SSD_GROUPS)]

    def scan_outputs(c, st, g):
        r0 = c * chunk
        rsl = pl.ds(base + r0, chunk)
        gl = g * GROUP_WIDTH
        acum, acum_t, dt_t, atot = st["acum"], st["acum_t"], st["dt_t"], st["atot"]
        cg = xbc_ref[rsl, cc0 + g * SSD_STATE:cc0 + (g + 1) * SSD_STATE]
        cbm = st["cbm"][g]
        ms = []
        for r in range(HEADS_PER_GROUP):
            hh = g * HEADS_PER_GROUP + r
            sgm = acum[:, hh:hh + 1] - acum_t[hh:hh + 1, :]
            dec = jnp.exp(jnp.where(causal, sgm, -jnp.inf))
            ms.append((cbm * dec * dt_t[hh:hh + 1, :]).astype(BF16))
        mg = jnp.concatenate(ms, axis=1)
        xg = st["xs_b"][:, gl:gl + GROUP_WIDTH]
        rhs = jnp.concatenate(
            [jnp.where(lane_head == r, xg, jnp.zeros_like(xg))
             for r in range(HEADS_PER_GROUP)], axis=0)
        y_diag = _dot(mg, rhs)

        for j in range(nseg):
            q0 = j * seg
            b_loc = (r0 + q0) // lb
            s_old = sout_ref[0, b_loc, gl:gl + GROUP_WIDTH, :]
            y_off = _dot_nt(cg[q0:q0 + seg].astype(BF16), s_old.astype(BF16))
            y_ref[pl.ds(base + r0 + q0, seg), gl:gl + GROUP_WIDTH] = (
                y_diag[q0:q0 + seg] + y_off * st["e_a"][q0:q0 + seg, gl:gl + GROUP_WIDTH])
            s_add = st["s_add"][g][j]
            q_last = q0 + seg - 1 if nseg > 1 else 0
            for r in range(HEADS_PER_GROUP):
                hh = g * HEADS_PER_GROUP + r
                keep = jnp.exp(atot[q_last:q_last + 1, hh:hh + 1])
                p0 = r * SSD_HEAD_DIM
                sout_ref[0, b_loc, gl + p0:gl + p0 + SSD_HEAD_DIM, :] = (
                    keep * s_old[p0:p0 + SSD_HEAD_DIM] + s_add[p0:p0 + SSD_HEAD_DIM])

    def before_scan():
        normed_input()
        dt_proj()
        for c0 in range(0, D_XBC, GROUP_WIDTH):
            conv_piece(c0)
            side_step()
        while queue:
            side_step()

    def scan():
        for c in range(n_chunks):
            st = scan_decays(c)
            scan_cb(c, st)
            scan_increments(c, st)
            for g in range(SSD_GROUPS):
                scan_outputs(c, st, g)

    def fused_step():
        normed_input()
        for c0 in range(0, D_XBC, GROUP_WIDTH):
            conv_piece(c0)
            side_step()
        dt_proj()
        sts = []
        for c in range(n_chunks):
            sts.append(scan_decays(c))
            scan_cb(c, sts[c])
            scan_increments(c, sts[c])
            side_step()
        for c in range(n_chunks):
            for g in range(SSD_GROUPS):
                scan_outputs(c, sts[c], g)
                side_step()
        after_scan()

    def after_scan():
        while queue:
            side_step()
        y = (y_ref[...] + xbc_ref[:, 0:SSD_INNER] * dexp_ref[0]) * get_side(0)
        parts = []
        for g in range(SSD_GROUPS):
            yg = y[:, g * GROUP_WIDTH:(g + 1) * GROUP_WIDTH]
            parts.append(yg * lax.rsqrt(jnp.mean(yg * yg, axis=-1, keepdims=True) + EPS))
        yn = jnp.concatenate(parts, axis=1) * snorm_ref[0]
        y_ssd = _dot(yn.astype(BF16), wos_ref[0])
        merged = get_side(1) * y_ssd + get_side(2)
        out = _dot(merged.astype(BF16), wo_ref[0]).reshape(nb, lb, D_MODEL)
        o_ref[...] = x_ref[...] + gt_ref[0] * out

    if single:
        fused_step()
    else:
        sub = lax.rem(t, nsub)
        pl.when(sub == 0)(before_scan)
        scan()
        pl.when(sub == nsub - 1)(after_scan)


def _mixer_call(x, mod, row0, layer, state_layer, s_in, bx_in, bs_in, lw, prev, *, nb, lb,
                chunk, nsub):
    NB, LB, d = x.shape
    rows = nb * lb
    sb = nb // nsub
    depth = lw["w_in_zx"].shape[0]
    assert nsub == 1 or LB == lb

    def per_tile(arr, lead):
        return pl.BlockSpec((1, nb) + arr.shape[2:], lambda b, t: (lead, b, 0, 0))

    def per_step(arr, lead):
        return pl.BlockSpec((1, sb) + arr.shape[2:],
                            lambda b, t: (lead, b * nsub + t % nsub, 0, 0))

    consts = [lw["norm_mix"], lw["w_in_zx"], lw["w_in_rest"], lw["w_in_dt"], lw["ssd_conv_w"],
              lw["ssd_conv_b"], lw["dt_bias"], lw["a_log"], lw["d_exp"], lw["ssd_norm"],
              lw["w_out_ssd"], lw["sc_conv_w"], lw["w_out_sc"], lw["w_o"]]
    state_shapes = [(depth,) + a.shape[1:] for a in (s_in, bx_in, bs_in)]
    n_in = 7 + len(consts)
    prev = list(prev) if prev is not None else []
    x_spec = pl.BlockSpec((nb, lb, d), lambda b, t: (b, t // nsub, 0))
    handover_rows = rows if nsub > 1 else SUBLANES
    return pl.pallas_call(
        functools.partial(_mixer_body, nb=nb, lb=lb, chunk=chunk, nsub=nsub, n_alias=len(prev)),
        grid=(NB // nb, (LB // lb) * nsub),
        in_specs=[
            x_spec,
            _mod_spec(nb, row0, layer, 3), _mod_spec(nb, row0, layer, 4),
            _mod_spec(nb, row0, layer, 5),
            per_step(s_in, state_layer), per_tile(bx_in, state_layer),
            per_tile(bs_in, state_layer),
        ] + [_layer_spec(w.shape, layer) for w in consts]
          + [pl.BlockSpec(memory_space=pl.ANY) for _ in prev],
        out_specs=[
            x_spec, per_step(s_in, layer), per_tile(bx_in, layer), per_tile(bs_in, layer),
        ],
        out_shape=[jax.ShapeDtypeStruct(x.shape, F32)]
                  + [jax.ShapeDtypeStruct(s, F32) for s in state_shapes],
        input_output_aliases={n_in + k: 1 + k for k in range(len(prev))},
        scratch_shapes=[
            pltpu.VMEM((nb, HALO + lb, D_XBC), F32),
            pltpu.VMEM((nb, HALO + lb, D_MODEL), F32),
            pltpu.VMEM((rows, D_XBC), F32),
            pltpu.VMEM((rows, SSD_INNER), F32),
            pltpu.VMEM((handover_rows, 2 * LANES), F32),
            pltpu.VMEM((handover_rows, 3 * D_MODEL), F32),
        ],
        compiler_params=pltpu.CompilerParams(
            dimension_semantics=("arbitrary", "arbitrary"),
            vmem_limit_bytes=VMEM_LIMIT),
        name="mixer",
    )(x, mod, mod, mod, s_in, bx_in, bs_in, *consts, *prev)


def _mixer_weights(w_in, norm_mix, ssd_conv_w, ssd_conv_b, ssd_dt_bias, ssd_a_log, ssd_d,
                   ssd_norm, w_out_ssd, sc_conv_w, w_out_sc, w_o):
    depth, d, _ = w_in.shape
    dt_lo = SSD_INNER + D_XBC
    dt_hi = dt_lo + SSD_HEADS
    pad = lambda v: jnp.pad(v, ((0, 0), (0, DT_PAD - SSD_HEADS))).reshape(depth, 1, DT_PAD)
    w_b = w_in.astype(BF16)
    return {
        "norm_mix": norm_mix.reshape(depth, 1, d),
        "w_in_zx": w_b[:, :, :dt_lo],
        "w_in_rest": w_b[:, :, dt_hi:],
        "w_in_dt": jnp.pad(w_b[:, :, dt_lo:dt_hi], ((0, 0), (0, 0), (0, DT_PAD - SSD_HEADS))),
        "ssd_conv_w": ssd_conv_w,
        "ssd_conv_b": ssd_conv_b.reshape(depth, 1, D_XBC),
        "dt_bias": pad(ssd_dt_bias),
        "a_log": pad(ssd_a_log),
        "d_exp": jnp.repeat(ssd_d, SSD_HEAD_DIM, axis=1).reshape(depth, 1, SSD_INNER),
        "ssd_norm": ssd_norm.reshape(depth, 1, SSD_INNER),
        "w_out_ssd": w_out_ssd.astype(BF16),
        "sc_conv_w": sc_conv_w,
        "w_out_sc": w_out_sc.astype(BF16),
        "w_o": w_o.astype(BF16),
    }


def kernel(x_prompt, x_sample, c_prompt, c_sample, state_ssm, state_conv_ssd, state_conv_short, w_ada, b_ada, norm_ffn1, norm_mix, norm_ffn2, ffn1_w_gu, ffn1_w_down, ffn2_w_gu, ffn2_w_down, w_in, ssd_conv_w, ssd_conv_b, ssd_dt_bias, ssd_a_log, ssd_d, ssd_norm, w_out_ssd, sc_conv_w, w_out_sc, w_o, norm_final):
    depth = w_ada.shape[0]
    bp, lp, d = x_prompt.shape
    bs, ls, _ = x_sample.shape
    hp = SSD_HEADS * SSD_HEAD_DIM

    mod = _mod_call(jnp.concatenate([c_sample, c_prompt], axis=0), w_ada, b_ada)
    row_s, row_p = 0, bs

    zeros_s = jnp.zeros((1, bp, hp, SSD_STATE), F32)
    zeros_bx = jnp.zeros((1, bp, SSD_CONV - 1, D_XBC), F32)
    zeros_bs = jnp.zeros((1, bp, SC_CONV - 1, D_MODEL), F32)
    state_s = state_ssm.reshape(depth, bs, hp, SSD_STATE)

    ffn_p = dict(nb=1, lb=min(2 * FFN_ROWS, lp))
    ffn_s = dict(nb=min(FFN_ROWS // ls, bs), lb=ls)
    mix_p = dict(nb=1, lb=min(MIX_ROWS_PROMPT, lp), chunk=min(SSD_CHUNK, lp), nsub=1)
    nb_s = min(MIX_BATCH_SAMPLE, bs)
    nsub_s = min(MIX_SUBSTEPS_SAMPLE, nb_s)
    mix_s = dict(nb=nb_s, lb=ls, chunk=nb_s // nsub_s * ls, nsub=nsub_s)

    n1 = norm_ffn1.reshape(depth, 1, d)
    n2 = norm_ffn2.reshape(depth, 1, d)
    gu1, dn1 = ffn1_w_gu.astype(BF16), ffn1_w_down.astype(BF16)
    gu2, dn2 = ffn2_w_gu.astype(BF16), ffn2_w_down.astype(BF16)
    lw = _mixer_weights(w_in, norm_mix, ssd_conv_w, ssd_conv_b, ssd_dt_bias, ssd_a_log, ssd_d,
                        ssd_norm, w_out_ssd, sc_conv_w, w_out_sc, w_o)

    xp, xs = x_prompt, x_sample
    st_p = st_s = None
    for l in range(depth):
        last = l == depth - 1
        xp = _ffn_call(xp, mod, row_p, 0, l, n1, gu1, dn1, norm_final, final_norm=False, **ffn_p)
        xs = _ffn_call(xs, mod, row_s, 0, l, n1, gu1, dn1, norm_final, final_norm=False, **ffn_s)

        xp, *st_p = _mixer_call(xp, mod, row_p, l, 0, zeros_s, zeros_bx, zeros_bs, lw, st_p,
                                **mix_p)
        xs, *st_s = _mixer_call(xs, mod, row_s, l, l, state_s, state_conv_ssd, state_conv_short,
                                lw, st_s, **mix_s)

        xp = _ffn_call(xp, mod, row_p, 6, l, n2, gu2, dn2, norm_final, final_norm=last, **ffn_p)
        xs = _ffn_call(xs, mod, row_s, 6, l, n2, gu2, dn2, norm_final, final_norm=last, **ffn_s)

    shp = (SSD_HEADS, SSD_HEAD_DIM, SSD_STATE)
    return (xp, xs,
            st_p[0].reshape((depth, bp) + shp), st_p[1], st_p[2],
            st_s[0].reshape((depth, bs) + shp), st_s[1], st_s[2])
```

```python
import functools

import jax
import jax.numpy as jnp
from jax import lax
from jax.experimental import pallas as pl
from jax.experimental.pallas import tpu as pltpu

F32 = jnp.float32
BF16 = jnp.bfloat16

D_MODEL = 1024
SSD_HEADS = 16
SSD_HEAD_DIM = 64
SSD_GROUPS = 4
SSD_STATE = 128
HEADS_PER_GROUP = SSD_HEADS // SSD_GROUPS
GROUP_WIDTH = HEADS_PER_GROUP * SSD_HEAD_DIM
SSD_INNER = SSD_HEADS * SSD_HEAD_DIM
D_XBC = SSD_INNER + 2 * SSD_GROUPS * SSD_STATE
SSD_CONV = 4
SC_CONV = 3
D_FF = 2816
N_MOD = 9
EPS = 1e-6

LANES = 128
SUBLANES = 8
HALO = SUBLANES
DT_PAD = LANES

C_Z = 0
C_XBC = C_Z + SSD_INNER
C_SCB = C_XBC + D_XBC
C_SCC = C_SCB + D_MODEL
C_SCH = C_SCC + D_MODEL
C_GSSD = C_SCH + D_MODEL
C_GSC = C_GSSD + D_MODEL
C_DT = C_GSC + D_MODEL

MOD_COLS = 2304
FF_CHUNK = 256
FFN_ROWS = 512
MIX_ROWS_PROMPT = 256
MIX_BATCH_SAMPLE = 32
MIX_SUBSTEPS_SAMPLE = 8
SSD_CHUNK = 128
VMEM_LIMIT = 56 * 1024 * 1024


def _dot(a, b):
    return jnp.dot(a, b, preferred_element_type=F32)


def _dot_nt(a, b):
    return lax.dot_general(a, b, (((1,), (1,)), ((), ())), preferred_element_type=F32)


def _dot_tn(a, b):
    return lax.dot_general(a, b, (((0,), (0,)), ((), ())), preferred_element_type=F32)


def _dot_exact_lhs(sel, x):
    hi = x.astype(BF16)
    r1 = x - hi.astype(F32)
    mid = r1.astype(BF16)
    lo = (r1 - mid.astype(F32)).astype(BF16)
    return _dot(sel, hi) + _dot(sel, mid) + _dot(sel, lo)


def _silu(x):
    return x / (1.0 + jnp.exp(-x))


def _sigmoid(x):
    return 1.0 / (1.0 + jnp.exp(-x))


def _softplus(x):
    return jnp.maximum(x, 0.0) + jnp.log1p(jnp.exp(-jnp.abs(x)))


def _rmsnorm(x, g):
    r = lax.rsqrt(jnp.mean(x * x, axis=-1, keepdims=True) + EPS)
    return (x * r) * g


def _layer_spec(shape, layer):
    nd = len(shape) - 1
    return pl.BlockSpec((1,) + tuple(shape[1:]), lambda *_: (layer,) + (0,) * nd,
                        pipeline_mode=pl.Buffered(1))


def _mod_spec(nb, row0, layer, k, single_buffer=False):
    assert row0 % nb == 0
    mode = dict(pipeline_mode=pl.Buffered(1)) if single_buffer else {}
    return pl.BlockSpec((1, nb, 1, D_MODEL), lambda b, i: (layer, row0 // nb + b, 0, k), **mode)


def _mod_body(c_ref, w_ref, b_ref, o_ref):
    sc = _silu(c_ref[...]).astype(BF16)
    o_ref[0, :, 0, :] = _dot(sc, w_ref[0].astype(BF16)) + b_ref[0]


def _mod_call(c_all, w_ada, b_ada):
    depth, d, n = w_ada.shape
    m = c_all.shape[0]
    tn = MOD_COLS
    return pl.pallas_call(
        _mod_body,
        grid=(depth, n // tn),
        in_specs=[
            pl.BlockSpec((m, d), lambda l, j: (0, 0)),
            pl.BlockSpec((1, d, tn), lambda l, j: (l, 0, j)),
            pl.BlockSpec((1, 1, tn), lambda l, j: (l, 0, j)),
        ],
        out_specs=pl.BlockSpec((1, m, 1, tn), lambda l, j: (l, 0, 0, j)),
        out_shape=jax.ShapeDtypeStruct((depth, m, 1, n), F32),
        compiler_params=pltpu.CompilerParams(
            dimension_semantics=("arbitrary", "arbitrary"),
            vmem_limit_bytes=VMEM_LIMIT),
        name="adaln_mod",
    )(c_all, w_ada, b_ada.reshape(depth, 1, n))


def _ffn_body(x_ref, xn_ref, sh_ref, sc_ref, gt_ref, shn_ref, scn_ref, nw_ref, wgu_ref, wd_ref,
              nf_ref, o_ref, act_ref, hb_ref, act0_ref, *, final_norm):
    nb, lb, d = x_ref.shape

    def normed(xr, shr, scr):
        h = _rmsnorm(xr[...], nw_ref[0]) * (1.0 + scr[0]) + shr[0]
        return h.reshape(nb * lb, d).astype(BF16)

    def act_chunk(hb, c):
        lo = c * FF_CHUNK
        g = _dot(hb, wgu_ref[0, :, lo:lo + FF_CHUNK])
        u = _dot(hb, wgu_ref[0, :, D_FF + lo:D_FF + lo + FF_CHUNK])
        return (_silu(g) * u).astype(BF16)

    @pl.when(jnp.logical_and(pl.program_id(0) == 0, pl.program_id(1) == 0))
    def _():
        hb0 = normed(x_ref, sh_ref, sc_ref)
        hb_ref[...] = hb0
        act0_ref[...] = act_chunk(hb0, 0)

    hb = hb_ref[...]
    act_ref[:, 0:FF_CHUNK] = act0_ref[...]
    n_chunks = D_FF // FF_CHUNK
    for c in range(1, n_chunks):
        act_ref[:, c * FF_CHUNK:(c + 1) * FF_CHUNK] = act_chunk(hb, c)
        if c == n_chunks // 2:
            hb_next = normed(xn_ref, shn_ref, scn_ref)
            hb_ref[...] = hb_next
            act0_ref[...] = act_chunk(hb_next, 0)
    y = _dot(act_ref[...], wd_ref[0])
    out = x_ref[...] + (0.5 * gt_ref[0]) * y.reshape(nb, lb, d)
    if final_norm:
        out = _rmsnorm(out, nf_ref[...])
    o_ref[...] = out


def _ffn_call(x, mod, row0, k_mod, layer, norm_w, w_gu, w_down, norm_final, *, nb, lb,
              final_norm):
    NB, LB, d = x.shape
    rows = nb * lb
    n_tiles = LB // lb
    last = (NB // nb) * n_tiles - 1

    def nxt(b, i):
        flat = jnp.minimum(b * n_tiles + i + 1, last)
        return flat // n_tiles, flat % n_tiles

    def mod_next(k):
        return pl.BlockSpec((1, nb, 1, d), lambda b, i: (layer, row0 // nb + nxt(b, i)[0], 0, k))

    return pl.pallas_call(
        functools.partial(_ffn_body, final_norm=final_norm),
        grid=(NB // nb, LB // lb),
        in_specs=[
            pl.BlockSpec((nb, lb, d), lambda b, i: (b, i, 0)),
            pl.BlockSpec((nb, lb, d), lambda b, i: nxt(b, i) + (0,)),
            _mod_spec(nb, row0, layer, k_mod),
            _mod_spec(nb, row0, layer, k_mod + 1),
            _mod_spec(nb, row0, layer, k_mod + 2),
            mod_next(k_mod), mod_next(k_mod + 1),
            _layer_spec(norm_w.shape, layer),
            _layer_spec(w_gu.shape, layer),
            _layer_spec(w_down.shape, layer),
            pl.BlockSpec((1, d), lambda b, i: (0, 0)),
        ],
        out_specs=pl.BlockSpec((nb, lb, d), lambda b, i: (b, i, 0)),
        out_shape=jax.ShapeDtypeStruct(x.shape, F32),
        scratch_shapes=[pltpu.VMEM((rows, D_FF), BF16), pltpu.VMEM((rows, d), BF16),
                        pltpu.VMEM((rows, FF_CHUNK), BF16)],
        compiler_params=pltpu.CompilerParams(
            dimension_semantics=("arbitrary", "arbitrary"),
            vmem_limit_bytes=VMEM_LIMIT),
        name="ffn",
    )(x, x, mod, mod, mod, mod, mod, norm_w, w_gu, w_down, norm_final.reshape(1, d))


def _causal_conv(ext_ref, cur, w_ref, new_buf_ref, lb, k_w, c0):
    cols = slice(c0, c0 + cur.shape[-1])
    ext_ref[:, HALO:HALO + lb, cols] = cur
    first = HALO - (k_w - 1)
    acc = None
    for k in range(k_w):
        tap = ext_ref[:, first + k:first + k + lb, cols] * w_ref[0, k:k + 1, cols]
        acc = tap if acc is None else acc + tap
    new_buf_ref[0, :, :, cols] = ext_ref[:, lb + first:lb + HALO, cols]
    ext_ref[:, 0:HALO, cols] = ext_ref[:, lb:lb + HALO, cols]
    return acc


def _mixer_body(x_ref, sh_ref, sc_ref, gt_ref, sin_ref, bxin_ref, bsin_ref,
                nw_ref, wzx_ref, wrest_ref, wdt_ref, cw_ref, cb_ref, dtb_ref, alog_ref,
                dexp_ref, snorm_ref, wos_ref, scw_ref, wosc_ref, wo_ref, *rest,
                nb, lb, chunk, nsub, n_alias):
    rest = rest[n_alias:]
    (o_ref, sout_ref, bxout_ref, bsout_ref,
     extx_ref, exts_ref, xbc_ref, y_ref, dtda_ref, side_ref) = rest
    rows = nb * lb
    sub_rows = rows // nsub
    seg = min(lb, chunk)
    nseg = chunk // seg
    seg_shift = seg.bit_length() - 1
    assert seg == 1 << seg_shift and sub_rows % chunk == 0 and rows % nsub == 0
    half = D_MODEL // 2
    single = nsub == 1
    t = pl.program_id(1)
    base = 0 if single else pl.multiple_of(lax.rem(t, nsub) * sub_rows, sub_rows)
    env = {}

    @pl.when(t < nsub)
    def _():
        sout_ref[0] = sin_ref[0]

    @pl.when(t == 0)
    def _():
        extx_ref[:, HALO - (SSD_CONV - 1):HALO, :] = bxin_ref[0]
        exts_ref[:, HALO - (SC_CONV - 1):HALO, :] = bsin_ref[0]

    def proj(lo, width):
        if lo >= C_DT:
            w_ref, lo = wdt_ref, lo - C_DT
        elif lo >= C_SCB:
            w_ref, lo = wrest_ref, lo - C_SCB
        else:
            w_ref = wzx_ref
        return _dot(env["hb"], w_ref[0, :, lo:lo + width])

    def put_side(k, hf, val):
        if single:
            env["side", k, hf] = val
        else:
            side_ref[:, k * D_MODEL + hf * half:k * D_MODEL + (hf + 1) * half] = val

    def get_side(k):
        if single:
            return jnp.concatenate([env["side", k, 0], env["side", k, 1]], axis=1)
        return side_ref[:, k * D_MODEL:(k + 1) * D_MODEL]

    def rows_of(name, col, r0):
        if single:
            return env[name][r0:r0 + chunk]
        return dtda_ref[pl.ds(base + r0, chunk), col * LANES:(col + 1) * LANES]

    def side_scc(hf):
        env["scc", hf] = proj(C_SCC + hf * half, half)

    def side_v(hf):
        env["v", hf] = env.pop(("scc", hf)) * proj(C_SCH + hf * half, half)

    def side_conv(_):
        v = jnp.concatenate([env.pop(("v", 0)), env.pop(("v", 1))], axis=1)
        env["u"] = _causal_conv(exts_ref, v.reshape(nb, lb, D_MODEL), scw_ref, bsout_ref, lb,
                                SC_CONV, 0).reshape(rows, D_MODEL)

    def side_su(hf):
        lo = hf * half
        env["su", hf] = (proj(C_SCB + lo, half) * env["u"][:, lo:lo + half]).astype(BF16)

    def side_ysc(hf):
        if hf == 0:
            env["su"] = jnp.concatenate([env.pop(("su", 0)), env.pop(("su", 1))], axis=1)
        env["ysc", hf] = _dot(env["su"], wosc_ref[0, :, hf * half:(hf + 1) * half])

    def side_gsc(hf):
        put_side(2, hf, _sigmoid(proj(C_GSC + hf * half, half)) * env.pop(("ysc", hf)))

    def side_gssd(hf):
        put_side(1, hf, _sigmoid(proj(C_GSSD + hf * half, half)))

    def side_z(hf):
        put_side(0, hf, _silu(proj(C_Z + hf * half, half)))

    queue = [functools.partial(f, hf) for f, hf in (
        (side_scc, 0), (side_v, 0), (side_scc, 1), (side_v, 1), (side_conv, 0),
        (side_su, 0), (side_su, 1), (side_ysc, 0), (side_ysc, 1), (side_gsc, 0), (side_gsc, 1),
        (side_gssd, 0), (side_gssd, 1), (side_z, 0), (side_z, 1))]

    def side_step():
        if queue:
            queue.pop(0)()

    ri = lax.broadcasted_iota(jnp.int32, (chunk, chunk), 0)
    ci = lax.broadcasted_iota(jnp.int32, (chunk, chunk), 1)
    same_seq = (ri >> seg_shift) == (ci >> seg_shift)
    causal = jnp.logical_and(same_seq, ci <= ri)
    causal_sel = causal.astype(F32).astype(BF16)
    seq_sel = same_seq.astype(F32).astype(BF16)
    eh = lax.broadcasted_iota(jnp.int32, (LANES, SSD_INNER), 0)
    ej = lax.broadcasted_iota(jnp.int32, (LANES, SSD_INNER), 1)
    head_sel = (eh == ej // SSD_HEAD_DIM).astype(F32).astype(BF16)
    lane_head = lax.broadcasted_iota(jnp.int32, (chunk, GROUP_WIDTH), 1) // SSD_HEAD_DIM
    bc0 = SSD_INNER
    cc0 = SSD_INNER + SSD_GROUPS * SSD_STATE
    n_chunks = sub_rows // chunk

    def normed_input():
        x = x_ref[...]
        h = _rmsnorm(x, nw_ref[0]) * (1.0 + sc_ref[0]) + sh_ref[0]
        env["hb"] = h.reshape(rows, D_MODEL).astype(BF16)

    def dt_proj():
        dt_all = _softplus(proj(C_DT, DT_PAD) + dtb_ref[0])
        da_all = dt_all * (-jnp.exp(alog_ref[0]))
        if single:
            env["dt"], env["da"] = dt_all, da_all
        else:
            dtda_ref[:, 0:LANES] = dt_all
            dtda_ref[:, LANES:2 * LANES] = da_all

    def conv_piece(c0):
        piece = proj(C_XBC + c0, GROUP_WIDTH).reshape(nb, lb, GROUP_WIDTH)
        piece = _causal_conv(extx_ref, piece, cw_ref, bxout_ref, lb, SSD_CONV, c0)
        piece = _silu(piece + cb_ref[0, :, c0:c0 + GROUP_WIDTH])
        xbc_ref[:, c0:c0 + GROUP_WIDTH] = piece.reshape(rows, GROUP_WIDTH)

    def scan_decays(c):
        r0 = c * chunk
        dt = rows_of("dt", 0, r0)
        da = rows_of("da", 1, r0)
        acum = _dot_exact_lhs(causal_sel, da)
        if nseg == 1:
            atot = acum[chunk - 1:chunk, :]
        else:
            atot = _dot_exact_lhs(seq_sel, da)
        ex = _dot(jnp.concatenate([dt * jnp.exp(atot - acum), jnp.exp(acum)],
                                  axis=0).astype(BF16), head_sel)
        return dict(acum=acum, atot=atot, to_end=ex[0:chunk], e_a=ex[chunk:2 * chunk],
                    acum_t=acum.T, dt_t=dt.T)

    def scan_cb(c, st):
        rsl = pl.ds(base + c * chunk, chunk)
        st["bg"] = [xbc_ref[rsl, bc0 + g * SSD_STATE:bc0 + (g + 1) * SSD_STATE].astype(BF16)
                    for g in range(SSD_GROUPS)]
        st["cbm"] = [
            _dot_nt(xbc_ref[rsl, cc0 + g * SSD_STATE:cc0 + (g + 1) * SSD_STATE].astype(BF16),
                    st["bg"][g]) for g in range(SSD_GROUPS)]

    def scan_increments(c, st):
        rsl = pl.ds(base + c * chunk, chunk)
        xs = xbc_ref[rsl, 0:SSD_INNER]
        xd = xs * st["to_end"]
        st["xs_b"] = xs.astype(BF16)
        st["s_add"] = [[
            _dot_tn(xd[j * seg:(j + 1) * seg, g * GROUP_WIDTH:(g + 1) * GROUP_WIDTH].astype(BF16),
                    st["bg"][g][j * seg:(j + 1) * seg]) for j in range(nseg)]
            for g in range(SSD_GROUPS)]

    def scan_outputs(c, st, g):
        r0 = c * chunk
        rsl = pl.ds(base + r0, chunk)
        gl = g * GROUP_WIDTH
        acum, acum_t, dt_t, atot = st["acum"], st["acum_t"], st["dt_t"], st["atot"]
        cg = xbc_ref[rsl, cc0 + g * SSD_STATE:cc0 + (g + 1) * SSD_STATE]
        cbm = st["cbm"][g]
        ms = []
        for r in range(HEADS_PER_GROUP):
            hh = g * HEADS_PER_GROUP + r
            sgm = acum[:, hh:hh + 1] - acum_t[hh:hh + 1, :]
            dec = jnp.exp(jnp.where(causal, sgm, -jnp.inf))
            ms.append((cbm * dec * dt_t[hh:hh + 1, :]).astype(BF16))
        mg = jnp.concatenate(ms, axis=1)
        xg = st["xs_b"][:, gl:gl + GROUP_WIDTH]
        rhs = jnp.concatenate(
            [jnp.where(lane_head == r, xg, jnp.zeros_like(xg))
             for r in range(HEADS_PER_GROUP)], axis=0)
        y_diag = _dot(mg, rhs)

        for j in range(nseg):
            q0 = j * seg
            b_loc = (r0 + q0) // lb
            s_old = sout_ref[0, b_loc, gl:gl + GROUP_WIDTH, :]
            y_off = _dot_nt(cg[q0:q0 + seg].astype(BF16), s_old.astype(BF16))
            y_ref[pl.ds(base + r0 + q0, seg), gl:gl + GROUP_WIDTH] = (
                y_diag[q0:q0 + seg] + y_off * st["e_a"][q0:q0 + seg, gl:gl + GROUP_WIDTH])
            s_add = st["s_add"][g][j]
            q_last = q0 + seg - 1 if nseg > 1 else 0
            for r in range(HEADS_PER_GROUP):
                hh = g * HEADS_PER_GROUP + r
                keep = jnp.exp(atot[q_last:q_last + 1, hh:hh + 1])
                p0 = r * SSD_HEAD_DIM
                sout_ref[0, b_loc, gl + p0:gl + p0 + SSD_HEAD_DIM, :] = (
                    keep * s_old[p0:p0 + SSD_HEAD_DIM] + s_add[p0:p0 + SSD_HEAD_DIM])

    def before_scan():
        normed_input()
        dt_proj()
        for c0 in range(0, D_XBC, GROUP_WIDTH):
            conv_piece(c0)
            side_step()
        while queue:
            side_step()

    def scan():
        for c in range(n_chunks):
            st = scan_decays(c)
            scan_cb(c, st)
            scan_increments(c, st)
            for g in range(SSD_GROUPS):
                scan_outputs(c, st, g)

    def fused_step():
        normed_input()
        for c0 in range(0, D_XBC, GROUP_WIDTH):
            conv_piece(c0)
            side_step()
        dt_proj()
        sts = []
        for c in range(n_chunks):
            sts.append(scan_decays(c))
            scan_cb(c, sts[c])
            scan_increments(c, sts[c])
            side_step()
        for c in range(n_chunks):
            for g in range(SSD_GROUPS):
                scan_outputs(c, sts[c], g)
                side_step()
        after_scan()

    def after_scan():
        while queue:
            side_step()
        y = (y_ref[...] + xbc_ref[:, 0:SSD_INNER] * dexp_ref[0]) * get_side(0)
        parts = []
        for g in range(SSD_GROUPS):
            yg = y[:, g * GROUP_WIDTH:(g + 1) * GROUP_WIDTH]
            parts.append(yg * lax.rsqrt(jnp.mean(yg * yg, axis=-1, keepdims=True) + EPS))
        yn = jnp.concatenate(parts, axis=1) * snorm_ref[0]
        y_ssd = _dot(yn.astype(BF16), wos_ref[0])
        merged = get_side(1) * y_ssd + get_side(2)
        out = _dot(merged.astype(BF16), wo_ref[0]).reshape(nb, lb, D_MODEL)
        o_ref[...] = x_ref[...] + gt_ref[0] * out

    if single:
        fused_step()
    else:
        sub = lax.rem(t, nsub)
        pl.when(sub == 0)(before_scan)
        scan()
        pl.when(sub == nsub - 1)(after_scan)


def _mixer_call(x, mod, row0, layer, state_layer, s_in, bx_in, bs_in, lw, prev, *, nb, lb,
                chunk, nsub):
    NB, LB, d = x.shape
    rows = nb * lb
    sb = nb // nsub
    depth = lw["w_in_zx"].shape[0]
    assert nsub == 1 or LB == lb

    def per_tile(arr, lead):
        return pl.BlockSpec((1, nb) + arr.shape[2:], lambda b, t: (lead, b, 0, 0))

    def per_step(arr, lead):
        return pl.BlockSpec((1, sb) + arr.shape[2:],
                            lambda b, t: (lead, b * nsub + t % nsub, 0, 0))

    consts = [lw["norm_mix"], lw["w_in_zx"], lw["w_in_rest"], lw["w_in_dt"], lw["ssd_conv_w"],
              lw["ssd_conv_b"], lw["dt_bias"], lw["a_log"], lw["d_exp"], lw["ssd_norm"],
              lw["w_out_ssd"], lw["sc_conv_w"], lw["w_out_sc"], lw["w_o"]]
    state_shapes = [(depth,) + a.shape[1:] for a in (s_in, bx_in, bs_in)]
    n_in = 7 + len(consts)
    prev = list(prev) if prev is not None else []
    x_spec = pl.BlockSpec((nb, lb, d), lambda b, t: (b, t // nsub, 0))
    handover_rows = rows if nsub > 1 else SUBLANES
    return pl.pallas_call(
        functools.partial(_mixer_body, nb=nb, lb=lb, chunk=chunk, nsub=nsub, n_alias=len(prev)),
        grid=(NB // nb, (LB // lb) * nsub),
        in_specs=[
            x_spec,
            _mod_spec(nb, row0, layer, 3, nsub > 1), _mod_spec(nb, row0, layer, 4, nsub > 1),
            _mod_spec(nb, row0, layer, 5, nsub > 1),
            per_step(s_in, state_layer), per_tile(bx_in, state_layer),
            per_tile(bs_in, state_layer),
        ] + [_layer_spec(w.shape, layer) for w in consts]
          + [pl.BlockSpec(memory_space=pl.ANY) for _ in prev],
        out_specs=[
            x_spec, per_step(s_in, layer), per_tile(bx_in, layer), per_tile(bs_in, layer),
        ],
        out_shape=[jax.ShapeDtypeStruct(x.shape, F32)]
                  + [jax.ShapeDtypeStruct(s, F32) for s in state_shapes],
        input_output_aliases={n_in + k: 1 + k for k in range(len(prev))},
        scratch_shapes=[
            pltpu.VMEM((nb, HALO + lb, D_XBC), F32),
            pltpu.VMEM((nb, HALO + lb, D_MODEL), F32),
            pltpu.VMEM((rows, D_XBC), F32),
            pltpu.VMEM((rows, SSD_INNER), F32),
            pltpu.VMEM((handover_rows, 2 * LANES), F32),
            pltpu.VMEM((handover_rows, 3 * D_MODEL), F32),
        ],
        compiler_params=pltpu.CompilerParams(
            dimension_semantics=("arbitrary", "arbitrary"),
            vmem_limit_bytes=VMEM_LIMIT),
        name="mixer",
    )(x, mod, mod, mod, s_in, bx_in, bs_in, *consts, *prev)


def _mixer_weights(w_in, norm_mix, ssd_conv_w, ssd_conv_b, ssd_dt_bias, ssd_a_log, ssd_d,
                   ssd_norm, w_out_ssd, sc_conv_w, w_out_sc, w_o):
    depth, d, _ = w_in.shape
    dt_lo = SSD_INNER + D_XBC
    dt_hi = dt_lo + SSD_HEADS
    pad = lambda v: jnp.pad(v, ((0, 0), (0, DT_PAD - SSD_HEADS))).reshape(depth, 1, DT_PAD)
    w_b = w_in.astype(BF16)
    return {
        "norm_mix": norm_mix.reshape(depth, 1, d),
        "w_in_zx": w_b[:, :, :dt_lo],
        "w_in_rest": w_b[:, :, dt_hi:],
        "w_in_dt": jnp.pad(w_b[:, :, dt_lo:dt_hi], ((0, 0), (0, 0), (0, DT_PAD - SSD_HEADS))),
        "ssd_conv_w": ssd_conv_w,
        "ssd_conv_b": ssd_conv_b.reshape(depth, 1, D_XBC),
        "dt_bias": pad(ssd_dt_bias),
        "a_log": pad(ssd_a_log),
        "d_exp": jnp.repeat(ssd_d, SSD_HEAD_DIM, axis=1).reshape(depth, 1, SSD_INNER),
        "ssd_norm": ssd_norm.reshape(depth, 1, SSD_INNER),
        "w_out_ssd": w_out_ssd.astype(BF16),
        "sc_conv_w": sc_conv_w,
        "w_out_sc": w_out_sc.astype(BF16),
        "w_o": w_o.astype(BF16),
    }


def kernel(x_prompt, x_sample, c_prompt, c_sample, state_ssm, state_conv_ssd, state_conv_short, w_ada, b_ada, norm_ffn1, norm_mix, norm_ffn2, ffn1_w_gu, ffn1_w_down, ffn2_w_gu, ffn2_w_down, w_in, ssd_conv_w, ssd_conv_b, ssd_dt_bias, ssd_a_log, ssd_d, ssd_norm, w_out_ssd, sc_conv_w, w_out_sc, w_o, norm_final):
    depth = w_ada.shape[0]
    bp, lp, d = x_prompt.shape
    bs, ls, _ = x_sample.shape
    hp = SSD_HEADS * SSD_HEAD_DIM

    mod = _mod_call(jnp.concatenate([c_sample, c_prompt], axis=0), w_ada, b_ada)
    row_s, row_p = 0, bs

    zeros_s = jnp.zeros((1, bp, hp, SSD_STATE), F32)
    zeros_bx = jnp.zeros((1, bp, SSD_CONV - 1, D_XBC), F32)
    zeros_bs = jnp.zeros((1, bp, SC_CONV - 1, D_MODEL), F32)
    state_s = state_ssm.reshape(depth, bs, hp, SSD_STATE)

    ffn_p = dict(nb=1, lb=min(2 * FFN_ROWS, lp))
    ffn_s = dict(nb=min(FFN_ROWS // ls, bs), lb=ls)
    mix_p = dict(nb=1, lb=min(MIX_ROWS_PROMPT, lp), chunk=min(SSD_CHUNK, lp), nsub=1)
    nb_s = min(MIX_BATCH_SAMPLE, bs)
    nsub_s = min(MIX_SUBSTEPS_SAMPLE, nb_s)
    mix_s = dict(nb=nb_s, lb=ls, chunk=nb_s // nsub_s * ls, nsub=nsub_s)

    n1 = norm_ffn1.reshape(depth, 1, d)
    n2 = norm_ffn2.reshape(depth, 1, d)
    gu1, dn1 = ffn1_w_gu.astype(BF16), ffn1_w_down.astype(BF16)
    gu2, dn2 = ffn2_w_gu.astype(BF16), ffn2_w_down.astype(BF16)
    lw = _mixer_weights(w_in, norm_mix, ssd_conv_w, ssd_conv_b, ssd_dt_bias, ssd_a_log, ssd_d,
                        ssd_norm, w_out_ssd, sc_conv_w, w_out_sc, w_o)

    xp, xs = x_prompt, x_sample
    st_p = st_s = None
    for l in range(depth):
        last = l == depth - 1
        xp = _ffn_call(xp, mod, row_p, 0, l, n1, gu1, dn1, norm_final, final_norm=False, **ffn_p)
        xs = _ffn_call(xs, mod, row_s, 0, l, n1, gu1, dn1, norm_final, final_norm=False, **ffn_s)

        xp, *st_p = _mixer_call(xp, mod, row_p, l, 0, zeros_s, zeros_bx, zeros_bs, lw, st_p,
                                **mix_p)
        xs, *st_s = _mixer_call(xs, mod, row_s, l, l, state_s, state_conv_ssd, state_conv_short,
                                lw, st_s, **mix_s)

        xp = _ffn_call(xp, mod, row_p, 6, l, n2, gu2, dn2, norm_final, final_norm=last, **ffn_p)
        xs = _ffn_call(xs, mod, row_s, 6, l, n2, gu2, dn2, norm_final, final_norm=last, **ffn_s)

    shp = (SSD_HEADS, SSD_HEAD_DIM, SSD_STATE)
    return (xp, xs,
            st_p[0].reshape((depth, bp) + shp), st_p[1], st_p[2],
            st_s[0].reshape((depth, bs) + shp), st_s[1], st_s[2])
```

```python
import functools

import jax
import jax.numpy as jnp
from jax import lax
from jax.experimental import pallas as pl
from jax.experimental.pallas import tpu as pltpu

F32 = jnp.float32
BF16 = jnp.bfloat16

D_MODEL = 1024
SSD_HEADS = 16
SSD_HEAD_DIM = 64
SSD_GROUPS = 4
SSD_STATE = 128
HEADS_PER_GROUP = SSD_HEADS // SSD_GROUPS
GROUP_WIDTH = HEADS_PER_GROUP * SSD_HEAD_DIM
SSD_INNER = SSD_HEADS * SSD_HEAD_DIM
D_XBC = SSD_INNER + 2 * SSD_GROUPS * SSD_STATE
SSD_CONV = 4
SC_CONV = 3
D_FF = 2816
N_MOD = 9
EPS = 1e-6

LANES = 128
SUBLANES = 8
HALO = SUBLANES
DT_PAD = LANES

C_Z = 0
C_XBC = C_Z + SSD_INNER
C_SCB = C_XBC + D_XBC
C_SCC = C_SCB + D_MODEL
C_SCH = C_SCC + D_MODEL
C_GSSD = C_SCH + D_MODEL
C_GSC = C_GSSD + D_MODEL
C_DT = C_GSC + D_MODEL

MOD_COLS = 2304
FF_CHUNK = 256
FFN_ROWS = 512
MIX_ROWS_PROMPT = 256
MIX_BATCH_SAMPLE = 16
MIX_SUBSTEPS_SAMPLE = 4
SSD_CHUNK = 128
STATE_RING = 3
VMEM_LIMIT = 56 * 1024 * 1024


def _dot(a, b):
    return jnp.dot(a, b, preferred_element_type=F32)


def _dot_nt(a, b):
    return lax.dot_general(a, b, (((1,), (1,)), ((), ())), preferred_element_type=F32)


def _dot_tn(a, b):
    return lax.dot_general(a, b, (((0,), (0,)), ((), ())), preferred_element_type=F32)


def _dot_exact_lhs(sel, x):
    hi = x.astype(BF16)
    r1 = x - hi.astype(F32)
    mid = r1.astype(BF16)
    lo = (r1 - mid.astype(F32)).astype(BF16)
    return _dot(sel, hi) + _dot(sel, mid) + _dot(sel, lo)


def _silu(x):
    return x / (1.0 + jnp.exp(-x))


def _sigmoid(x):
    return 1.0 / (1.0 + jnp.exp(-x))


def _softplus(x):
    return jnp.maximum(x, 0.0) + jnp.log1p(jnp.exp(-jnp.abs(x)))


def _rmsnorm(x, g):
    r = lax.rsqrt(jnp.mean(x * x, axis=-1, keepdims=True) + EPS)
    return (x * r) * g


def _layer_spec(shape, layer):
    nd = len(shape) - 1
    return pl.BlockSpec((1,) + tuple(shape[1:]), lambda *_: (layer,) + (0,) * nd,
                        pipeline_mode=pl.Buffered(1))


def _mod_spec(nb, row0, layer, k):
    assert row0 % nb == 0
    return pl.BlockSpec((1, nb, 1, D_MODEL), lambda b, i: (layer, row0 // nb + b, 0, k))


def _mod_body(c_ref, w_ref, b_ref, o_ref):
    sc = _silu(c_ref[...]).astype(BF16)
    o_ref[0, :, 0, :] = _dot(sc, w_ref[0].astype(BF16)) + b_ref[0]


def _mod_call(c_all, w_ada, b_ada):
    depth, d, n = w_ada.shape
    m = c_all.shape[0]
    tn = MOD_COLS
    return pl.pallas_call(
        _mod_body,
        grid=(depth, n // tn),
        in_specs=[
            pl.BlockSpec((m, d), lambda l, j: (0, 0)),
            pl.BlockSpec((1, d, tn), lambda l, j: (l, 0, j)),
            pl.BlockSpec((1, 1, tn), lambda l, j: (l, 0, j)),
        ],
        out_specs=pl.BlockSpec((1, m, 1, tn), lambda l, j: (l, 0, 0, j)),
        out_shape=jax.ShapeDtypeStruct((depth, m, 1, n), F32),
        compiler_params=pltpu.CompilerParams(
            dimension_semantics=("arbitrary", "arbitrary"),
            vmem_limit_bytes=VMEM_LIMIT),
        name="adaln_mod",
    )(c_all, w_ada, b_ada.reshape(depth, 1, n))


def _ffn_body(x_ref, xn_ref, sh_ref, sc_ref, gt_ref, shn_ref, scn_ref, nw_ref, wgu_ref, wd_ref,
              nf_ref, o_ref, act_ref, hb_ref, act0_ref, *, final_norm):
    nb, lb, d = x_ref.shape

    def normed(xr, shr, scr):
        h = _rmsnorm(xr[...], nw_ref[0]) * (1.0 + scr[0]) + shr[0]
        return h.reshape(nb * lb, d).astype(BF16)

    def act_chunk(hb, c):
        lo = c * FF_CHUNK
        g = _dot(hb, wgu_ref[0, :, lo:lo + FF_CHUNK])
        u = _dot(hb, wgu_ref[0, :, D_FF + lo:D_FF + lo + FF_CHUNK])
        return (_silu(g) * u).astype(BF16)

    @pl.when(jnp.logical_and(pl.program_id(0) == 0, pl.program_id(1) == 0))
    def _():
        hb0 = normed(x_ref, sh_ref, sc_ref)
        hb_ref[...] = hb0
        act0_ref[...] = act_chunk(hb0, 0)

    hb = hb_ref[...]
    act_ref[:, 0:FF_CHUNK] = act0_ref[...]
    n_chunks = D_FF // FF_CHUNK
    for c in range(1, n_chunks):
        act_ref[:, c * FF_CHUNK:(c + 1) * FF_CHUNK] = act_chunk(hb, c)
        if c == n_chunks // 2:
            hb_next = normed(xn_ref, shn_ref, scn_ref)
            hb_ref[...] = hb_next
            act0_ref[...] = act_chunk(hb_next, 0)
    y = _dot(act_ref[...], wd_ref[0])
    out = x_ref[...] + (0.5 * gt_ref[0]) * y.reshape(nb, lb, d)
    if final_norm:
        out = _rmsnorm(out, nf_ref[...])
    o_ref[...] = out


def _ffn_call(x, mod, row0, k_mod, layer, norm_w, w_gu, w_down, norm_final, *, nb, lb,
              final_norm):
    NB, LB, d = x.shape
    rows = nb * lb
    n_tiles = LB // lb
    last = (NB // nb) * n_tiles - 1

    def nxt(b, i):
        flat = jnp.minimum(b * n_tiles + i + 1, last)
        return flat // n_tiles, flat % n_tiles

    def mod_next(k):
        return pl.BlockSpec((1, nb, 1, d), lambda b, i: (layer, row0 // nb + nxt(b, i)[0], 0, k))

    return pl.pallas_call(
        functools.partial(_ffn_body, final_norm=final_norm),
        grid=(NB // nb, LB // lb),
        in_specs=[
            pl.BlockSpec((nb, lb, d), lambda b, i: (b, i, 0)),
            pl.BlockSpec((nb, lb, d), lambda b, i: nxt(b, i) + (0,)),
            _mod_spec(nb, row0, layer, k_mod),
            _mod_spec(nb, row0, layer, k_mod + 1),
            _mod_spec(nb, row0, layer, k_mod + 2),
            mod_next(k_mod), mod_next(k_mod + 1),
            _layer_spec(norm_w.shape, layer),
            _layer_spec(w_gu.shape, layer),
            _layer_spec(w_down.shape, layer),
            pl.BlockSpec((1, d), lambda b, i: (0, 0)),
        ],
        out_specs=pl.BlockSpec((nb, lb, d), lambda b, i: (b, i, 0)),
        out_shape=jax.ShapeDtypeStruct(x.shape, F32),
        scratch_shapes=[pltpu.VMEM((rows, D_FF), BF16), pltpu.VMEM((rows, d), BF16),
                        pltpu.VMEM((rows, FF_CHUNK), BF16)],
        compiler_params=pltpu.CompilerParams(
            dimension_semantics=("arbitrary", "arbitrary"),
            vmem_limit_bytes=VMEM_LIMIT),
        name="ffn",
    )(x, x, mod, mod, mod, mod, mod, norm_w, w_gu, w_down, norm_final.reshape(1, d))


def _causal_conv(ext_ref, cur, w_ref, new_buf_ref, lb, k_w, c0):
    cols = slice(c0, c0 + cur.shape[-1])
    ext_ref[:, HALO:HALO + lb, cols] = cur
    first = HALO - (k_w - 1)
    acc = None
    for k in range(k_w):
        tap = ext_ref[:, first + k:first + k + lb, cols] * w_ref[0, k:k + 1, cols]
        acc = tap if acc is None else acc + tap
    new_buf_ref[0, :, :, cols] = ext_ref[:, lb + first:lb + HALO, cols]
    ext_ref[:, 0:HALO, cols] = ext_ref[:, lb:lb + HALO, cols]
    return acc


def _mixer_body(x_ref, sh_ref, sc_ref, gt_ref, sin_ref, bxin_ref, bsin_ref,
                nw_ref, wzx_ref, wrest_ref, wdt_ref, cw_ref, cb_ref, dtb_ref, alog_ref,
                dexp_ref, snorm_ref, wos_ref, scw_ref, wosc_ref, wo_ref, *rest,
                nb, lb, chunk, nsub, n_alias, state_layer):
    rest = rest[n_alias:]
    (o_ref, sout_ref, bxout_ref, bsout_ref,
     extx_ref, exts_ref, xbc_ref, y_ref, dtda_ref, side_ref) = rest[:10]
    rows = nb * lb
    sub_rows = rows // nsub
    seg = min(lb, chunk)
    nseg = chunk // seg
    seg_shift = seg.bit_length() - 1
    assert seg == 1 << seg_shift and sub_rows % chunk == 0 and rows % nsub == 0
    half = D_MODEL // 2
    single = nsub == 1
    t = pl.program_id(1)
    base = 0 if single else pl.multiple_of(lax.rem(t, nsub) * sub_rows, sub_rows)
    env = {}

    if single:
        @pl.when(t < nsub)
        def _():
            sout_ref[0] = sin_ref[0]
    else:
        sbuf_ref, ssem_ref = rest[10:]
        sb = nb // nsub
        step = pl.program_id(0) * nsub + t
        n_steps = pl.num_programs(0) * nsub

        def state_copy(s):
            slot = lax.rem(s, STATE_RING)
            return pltpu.make_async_copy(
                sin_ref.at[state_layer, pl.ds(s * sb, sb)], sbuf_ref.at[slot], ssem_ref.at[slot])

        @pl.when(step == 0)
        def _():
            for s in range(STATE_RING - 1):
                state_copy(s).start()

        @pl.when(step + STATE_RING - 1 < n_steps)
        def _():
            state_copy(step + STATE_RING - 1).start()

        state_copy(step).wait()
        sout_ref[0] = sbuf_ref[lax.rem(step, STATE_RING)]

    @pl.when(t == 0)
    def _():
        extx_ref[:, HALO - (SSD_CONV - 1):HALO, :] = bxin_ref[0]
        exts_ref[:, HALO - (SC_CONV - 1):HALO, :] = bsin_ref[0]

    def proj(lo, width):
        if lo >= C_DT:
            w_ref, lo = wdt_ref, lo - C_DT
        elif lo >= C_SCB:
            w_ref, lo = wrest_ref, lo - C_SCB
        else:
            w_ref = wzx_ref
        return _dot(env["hb"], w_ref[0, :, lo:lo + width])

    def put_side(k, hf, val):
        if single:
            env["side", k, hf] = val
        else:
            side_ref[:, k * D_MODEL + hf * half:k * D_MODEL + (hf + 1) * half] = val

    def get_side(k):
        if single:
            return jnp.concatenate([env["side", k, 0], env["side", k, 1]], axis=1)
        return side_ref[:, k * D_MODEL:(k + 1) * D_MODEL]

    def rows_of(name, col, r0):
        if single:
            return env[name][r0:r0 + chunk]
        return dtda_ref[pl.ds(base + r0, chunk), col * LANES:(col + 1) * LANES]

    def side_scc(hf):
        env["scc", hf] = proj(C_SCC + hf * half, half)

    def side_v(hf):
        env["v", hf] = env.pop(("scc", hf)) * proj(C_SCH + hf * half, half)

    def side_conv(_):
        v = jnp.concatenate([env.pop(("v", 0)), env.pop(("v", 1))], axis=1)
        env["u"] = _causal_conv(exts_ref, v.reshape(nb, lb, D_MODEL), scw_ref, bsout_ref, lb,
                                SC_CONV, 0).reshape(rows, D_MODEL)

    def side_su(hf):
        lo = hf * half
        env["su", hf] = (proj(C_SCB + lo, half) * env["u"][:, lo:lo + half]).astype(BF16)

    def side_ysc(hf):
        if hf == 0:
            env["su"] = jnp.concatenate([env.pop(("su", 0)), env.pop(("su", 1))], axis=1)
        env["ysc", hf] = _dot(env["su"], wosc_ref[0, :, hf * half:(hf + 1) * half])

    def side_gsc(hf):
        put_side(2, hf, _sigmoid(proj(C_GSC + hf * half, half)) * env.pop(("ysc", hf)))

    def side_gssd(hf):
        put_side(1, hf, _sigmoid(proj(C_GSSD + hf * half, half)))

    def side_z(hf):
        put_side(0, hf, _silu(proj(C_Z + hf * half, half)))

    queue = [functools.partial(f, hf) for f, hf in (
        (side_scc, 0), (side_v, 0), (side_scc, 1), (side_v, 1), (side_conv, 0),
        (side_su, 0), (side_su, 1), (side_ysc, 0), (side_ysc, 1), (side_gsc, 0), (side_gsc, 1),
        (side_gssd, 0), (side_gssd, 1), (side_z, 0), (side_z, 1))]

    def side_step():
        if queue:
            queue.pop(0)()

    ri = lax.broadcasted_iota(jnp.int32, (chunk, chunk), 0)
    ci = lax.broadcasted_iota(jnp.int32, (chunk, chunk), 1)
    same_seq = (ri >> seg_shift) == (ci >> seg_shift)
    causal = jnp.logical_and(same_seq, ci <= ri)
    causal_sel = causal.astype(F32).astype(BF16)
    seq_sel = same_seq.astype(F32).astype(BF16)
    eh = lax.broadcasted_iota(jnp.int32, (LANES, SSD_INNER), 0)
    ej = lax.broadcasted_iota(jnp.int32, (LANES, SSD_INNER), 1)
    head_sel = (eh == ej // SSD_HEAD_DIM).astype(F32).astype(BF16)
    lane_head = lax.broadcasted_iota(jnp.int32, (chunk, GROUP_WIDTH), 1) // SSD_HEAD_DIM
    bc0 = SSD_INNER
    cc0 = SSD_INNER + SSD_GROUPS * SSD_STATE
    n_chunks = sub_rows // chunk

    def normed_input():
        x = x_ref[...]
        h = _rmsnorm(x, nw_ref[0]) * (1.0 + sc_ref[0]) + sh_ref[0]
        env["hb"] = h.reshape(rows, D_MODEL).astype(BF16)

    def dt_proj():
        dt_all = _softplus(proj(C_DT, DT_PAD) + dtb_ref[0])
        da_all = dt_all * (-jnp.exp(alog_ref[0]))
        if single:
            env["dt"], env["da"] = dt_all, da_all
        else:
            dtda_ref[:, 0:LANES] = dt_all
            dtda_ref[:, LANES:2 * LANES] = da_all

    def conv_piece(c0):
        piece = proj(C_XBC + c0, GROUP_WIDTH).reshape(nb, lb, GROUP_WIDTH)
        piece = _causal_conv(extx_ref, piece, cw_ref, bxout_ref, lb, SSD_CONV, c0)
        piece = _silu(piece + cb_ref[0, :, c0:c0 + GROUP_WIDTH])
        xbc_ref[:, c0:c0 + GROUP_WIDTH] = piece.reshape(rows, GROUP_WIDTH)

    def scan_decays(c):
        r0 = c * chunk
        dt = rows_of("dt", 0, r0)
        da = rows_of("da", 1, r0)
        acum = _dot_exact_lhs(causal_sel, da)
        if nseg == 1:
            atot = acum[chunk - 1:chunk, :]
        else:
            atot = _dot_exact_lhs(seq_sel, da)
        ex = _dot(jnp.concatenate([dt * jnp.exp(atot - acum), jnp.exp(acum)],
                                  axis=0).astype(BF16), head_sel)
        return dict(acum=acum, atot=atot, to_end=ex[0:chunk], e_a=ex[chunk:2 * chunk],
                    acum_t=acum.T, dt_t=dt.T)

    def scan_cb(c, st):
        rsl = pl.ds(base + c * chunk, chunk)
        st["bg"] = [xbc_ref[rsl, bc0 + g * SSD_STATE:bc0 + (g + 1) * SSD_STATE].astype(BF16)
                    for g in range(SSD_GROUPS)]
        st["cbm"] = [
            _dot_nt(xbc_ref[rsl, cc0 + g * SSD_STATE:cc0 + (g + 1) * SSD_STATE].astype(BF16),
                    st["bg"][g]) for g in range(SSD_GROUPS)]

    def scan_increments(c, st):
        rsl = pl.ds(base + c * chunk, chunk)
        xs = xbc_ref[rsl, 0:SSD_INNER]
        xd = xs * st["to_end"]
        st["xs_b"] = xs.astype(BF16)
        st["s_add"] = [[
            _dot_tn(xd[j * seg:(j + 1) * seg, g * GROUP_WIDTH:(g + 1) * GROUP_WIDTH].astype(BF16),
                    st["bg"][g][j * seg:(j + 1) * seg]) for j in range(nseg)]
            for g in range(SSD_GROUPS)]

    def scan_outputs(c, st, g):
        r0 = c * chunk
        rsl = pl.ds(base + r0, chunk)
        gl = g * GROUP_WIDTH
        acum, acum_t, dt_t, atot = st["acum"], st["acum_t"], st["dt_t"], st["atot"]
        cg = xbc_ref[rsl, cc0 + g * SSD_STATE:cc0 + (g + 1) * SSD_STATE]
        cbm = st["cbm"][g]
        ms = []
        for r in range(HEADS_PER_GROUP):
            hh = g * HEADS_PER_GROUP + r
            sgm = acum[:, hh:hh + 1] - acum_t[hh:hh + 1, :]
            dec = jnp.exp(jnp.where(causal, sgm, -jnp.inf))
            ms.append((cbm * dec * dt_t[hh:hh + 1, :]).astype(BF16))
        mg = jnp.concatenate(ms, axis=1)
        xg = st["xs_b"][:, gl:gl + GROUP_WIDTH]
        rhs = jnp.concatenate(
            [jnp.where(lane_head == r, xg, jnp.zeros_like(xg))
             for r in range(HEADS_PER_GROUP)], axis=0)
        y_diag = _dot(mg, rhs)

        for j in range(nseg):
            q0 = j * seg
            b_loc = (r0 + q0) // lb
            s_old = sout_ref[0, b_loc, gl:gl + GROUP_WIDTH, :]
            y_off = _dot_nt(cg[q0:q0 + seg].astype(BF16), s_old.astype(BF16))
            y_ref[pl.ds(base + r0 + q0, seg), gl:gl + GROUP_WIDTH] = (
                y_diag[q0:q0 + seg] + y_off * st["e_a"][q0:q0 + seg, gl:gl + GROUP_WIDTH])
            s_add = st["s_add"][g][j]
            q_last = q0 + seg - 1 if nseg > 1 else 0
            for r in range(HEADS_PER_GROUP):
                hh = g * HEADS_PER_GROUP + r
                keep = jnp.exp(atot[q_last:q_last + 1, hh:hh + 1])
                p0 = r * SSD_HEAD_DIM
                sout_ref[0, b_loc, gl + p0:gl + p0 + SSD_HEAD_DIM, :] = (
                    keep * s_old[p0:p0 + SSD_HEAD_DIM] + s_add[p0:p0 + SSD_HEAD_DIM])

    def before_scan():
        normed_input()
        dt_proj()
        for c0 in range(0, D_XBC, GROUP_WIDTH):
            conv_piece(c0)
            side_step()
        while queue:
            side_step()

    def scan():
        for c in range(n_chunks):
            st = scan_decays(c)
            scan_cb(c, st)
            scan_increments(c, st)
            for g in range(SSD_GROUPS):
                scan_outputs(c, st, g)

    def fused_step():
        normed_input()
        for c0 in range(0, D_XBC, GROUP_WIDTH):
            conv_piece(c0)
            side_step()
        dt_proj()
        sts = []
        for c in range(n_chunks):
            sts.append(scan_decays(c))
            scan_cb(c, sts[c])
            scan_increments(c, sts[c])
            side_step()
        for c in range(n_chunks):
            for g in range(SSD_GROUPS):
                scan_outputs(c, sts[c], g)
                side_step()
        after_scan()

    def after_scan():
        while queue:
            side_step()
        y = (y_ref[...] + xbc_ref[:, 0:SSD_INNER] * dexp_ref[0]) * get_side(0)
        parts = []
        for g in range(SSD_GROUPS):
            yg = y[:, g * GROUP_WIDTH:(g + 1) * GROUP_WIDTH]
            parts.append(yg * lax.rsqrt(jnp.mean(yg * yg, axis=-1, keepdims=True) + EPS))
        yn = jnp.concatenate(parts, axis=1) * snorm_ref[0]
        y_ssd = _dot(yn.astype(BF16), wos_ref[0])
        merged = get_side(1) * y_ssd + get_side(2)
        out = _dot(merged.astype(BF16), wo_ref[0]).reshape(nb, lb, D_MODEL)
        o_ref[...] = x_ref[...] + gt_ref[0] * out

    if single:
        fused_step()
    else:
        sub = lax.rem(t, nsub)
        pl.when(sub == 0)(before_scan)
        scan()
        pl.when(sub == nsub - 1)(after_scan)


def _mixer_call(x, mod, row0, layer, state_layer, s_in, bx_in, bs_in, lw, prev, *, nb, lb,
                chunk, nsub):
    NB, LB, d = x.shape
    rows = nb * lb
    sb = nb // nsub
    depth = lw["w_in_zx"].shape[0]
    assert nsub == 1 or LB == lb

    def per_tile(arr, lead):
        return pl.BlockSpec((1, nb) + arr.shape[2:], lambda b, t: (lead, b, 0, 0))

    def per_step(arr, lead):
        return pl.BlockSpec((1, sb) + arr.shape[2:],
                            lambda b, t: (lead, b * nsub + t % nsub, 0, 0))

    consts = [lw["norm_mix"], lw["w_in_zx"], lw["w_in_rest"], lw["w_in_dt"], lw["ssd_conv_w"],
              lw["ssd_conv_b"], lw["dt_bias"], lw["a_log"], lw["d_exp"], lw["ssd_norm"],
              lw["w_out_ssd"], lw["sc_conv_w"], lw["w_out_sc"], lw["w_o"]]
    state_shapes = [(depth,) + a.shape[1:] for a in (s_in, bx_in, bs_in)]
    n_in = 7 + len(consts)
    prev = list(prev) if prev is not None else []
    x_spec = pl.BlockSpec((nb, lb, d), lambda b, t: (b, t // nsub, 0))
    handover_rows = rows if nsub > 1 else SUBLANES
    return pl.pallas_call(
        functools.partial(_mixer_body, nb=nb, lb=lb, chunk=chunk, nsub=nsub, n_alias=len(prev),
                          state_layer=state_layer),
        grid=(NB // nb, (LB // lb) * nsub),
        in_specs=[
            x_spec,
            _mod_spec(nb, row0, layer, 3), _mod_spec(nb, row0, layer, 4),
            _mod_spec(nb, row0, layer, 5),
            per_step(s_in, state_layer) if nsub == 1 else pl.BlockSpec(memory_space=pl.ANY),
            per_tile(bx_in, state_layer),
            per_tile(bs_in, state_layer),
        ] + [_layer_spec(w.shape, layer) for w in consts]
          + [pl.BlockSpec(memory_space=pl.ANY) for _ in prev],
        out_specs=[
            x_spec, per_step(s_in, layer), per_tile(bx_in, layer), per_tile(bs_in, layer),
        ],
        out_shape=[jax.ShapeDtypeStruct(x.shape, F32)]
                  + [jax.ShapeDtypeStruct(s, F32) for s in state_shapes],
        input_output_aliases={n_in + k: 1 + k for k in range(len(prev))},
        scratch_shapes=[
            pltpu.VMEM((nb, HALO + lb, D_XBC), F32),
            pltpu.VMEM((nb, HALO + lb, D_MODEL), F32),
            pltpu.VMEM((rows, D_XBC), F32),
            pltpu.VMEM((rows, SSD_INNER), F32),
            pltpu.VMEM((handover_rows, 2 * LANES), F32),
            pltpu.VMEM((handover_rows, 3 * D_MODEL), F32),
        ] + ([] if nsub == 1 else [
            pltpu.VMEM((STATE_RING, sb) + s_in.shape[2:], F32),
            pltpu.SemaphoreType.DMA((STATE_RING,)),
        ]),
        compiler_params=pltpu.CompilerParams(
            dimension_semantics=("arbitrary", "arbitrary"),
            vmem_limit_bytes=VMEM_LIMIT),
        name="mixer",
    )(x, mod, mod, mod, s_in, bx_in, bs_in, *consts, *prev)


def _mixer_weights(w_in, norm_mix, ssd_conv_w, ssd_conv_b, ssd_dt_bias, ssd_a_log, ssd_d,
                   ssd_norm, w_out_ssd, sc_conv_w, w_out_sc, w_o):
    depth, d, _ = w_in.shape
    dt_lo = SSD_INNER + D_XBC
    dt_hi = dt_lo + SSD_HEADS
    pad = lambda v: jnp.pad(v, ((0, 0), (0, DT_PAD - SSD_HEADS))).reshape(depth, 1, DT_PAD)
    w_b = w_in.astype(BF16)
    return {
        "norm_mix": norm_mix.reshape(depth, 1, d),
        "w_in_zx": w_b[:, :, :dt_lo],
        "w_in_rest": w_b[:, :, dt_hi:],
        "w_in_dt": jnp.pad(w_b[:, :, dt_lo:dt_hi], ((0, 0), (0, 0), (0, DT_PAD - SSD_HEADS))),
        "ssd_conv_w": ssd_conv_w,
        "ssd_conv_b": ssd_conv_b.reshape(depth, 1, D_XBC),
        "dt_bias": pad(ssd_dt_bias),
        "a_log": pad(ssd_a_log),
        "d_exp": jnp.repeat(ssd_d, SSD_HEAD_DIM, axis=1).reshape(depth, 1, SSD_INNER),
        "ssd_norm": ssd_norm.reshape(depth, 1, SSD_INNER),
        "w_out_ssd": w_out_ssd.astype(BF16),
        "sc_conv_w": sc_conv_w,
        "w_out_sc": w_out_sc.astype(BF16),
        "w_o": w_o.astype(BF16),
    }


def kernel(x_prompt, x_sample, c_prompt, c_sample, state_ssm, state_conv_ssd, state_conv_short, w_ada, b_ada, norm_ffn1, norm_mix, norm_ffn2, ffn1_w_gu, ffn1_w_down, ffn2_w_gu, ffn2_w_down, w_in, ssd_conv_w, ssd_conv_b, ssd_dt_bias, ssd_a_log, ssd_d, ssd_norm, w_out_ssd, sc_conv_w, w_out_sc, w_o, norm_final):
    depth = w_ada.shape[0]
    bp, lp, d = x_prompt.shape
    bs, ls, _ = x_sample.shape
    hp = SSD_HEADS * SSD_HEAD_DIM

    mod = _mod_call(jnp.concatenate([c_sample, c_prompt], axis=0), w_ada, b_ada)
    row_s, row_p = 0, bs

    zeros_s = jnp.zeros((1, bp, hp, SSD_STATE), F32)
    zeros_bx = jnp.zeros((1, bp, SSD_CONV - 1, D_XBC), F32)
    zeros_bs = jnp.zeros((1, bp, SC_CONV - 1, D_MODEL), F32)
    state_s = state_ssm.reshape(depth, bs, hp, SSD_STATE)

    ffn_p = dict(nb=1, lb=min(2 * FFN_ROWS, lp))
    ffn_s = dict(nb=min(FFN_ROWS // ls, bs), lb=ls)
    mix_p = dict(nb=1, lb=min(MIX_ROWS_PROMPT, lp), chunk=min(SSD_CHUNK, lp), nsub=1)
    nb_s = min(MIX_BATCH_SAMPLE, bs)
    nsub_s = min(MIX_SUBSTEPS_SAMPLE, nb_s)
    mix_s = dict(nb=nb_s, lb=ls, chunk=nb_s // nsub_s * ls, nsub=nsub_s)

    n1 = norm_ffn1.reshape(depth, 1, d)
    n2 = norm_ffn2.reshape(depth, 1, d)
    gu1, dn1 = ffn1_w_gu.astype(BF16), ffn1_w_down.astype(BF16)
    gu2, dn2 = ffn2_w_gu.astype(BF16), ffn2_w_down.astype(BF16)
    lw = _mixer_weights(w_in, norm_mix, ssd_conv_w, ssd_conv_b, ssd_dt_bias, ssd_a_log, ssd_d,
                        ssd_norm, w_out_ssd, sc_conv_w, w_out_sc, w_o)

    xp, xs = x_prompt, x_sample
    st_p = st_s = None
    for l in range(depth):
        last = l == depth - 1
        xp = _ffn_call(xp, mod, row_p, 0, l, n1, gu1, dn1, norm_final, final_norm=False, **ffn_p)
        xs = _ffn_call(xs, mod, row_s, 0, l, n1, gu1, dn1, norm_final, final_norm=False, **ffn_s)

        xp, *st_p = _mixer_call(xp, mod, row_p, l, 0, zeros_s, zeros_bx, zeros_bs, lw, st_p,
                                **mix_p)
        xs, *st_s = _mixer_call(xs, mod, row_s, l, l, state_s, state_conv_ssd, state_conv_short,
                                lw, st_s, **mix_s)

        xp = _ffn_call(xp, mod, row_p, 6, l, n2, gu2, dn2, norm_final, final_norm=last, **ffn_p)
        xs = _ffn_call(xs, mod, row_s, 6, l, n2, gu2, dn2, norm_final, final_norm=last, **ffn_s)

    shp = (SSD_HEADS, SSD_HEAD_DIM, SSD_STATE)
    return (xp, xs,
            st_p[0].reshape((depth, bp) + shp), st_p[1], st_p[2],
            st_s[0].reshape((depth, bs) + shp), st_s[1], st_s[2])
```

```python
import functools

import jax
import jax.numpy as jnp
from jax import lax
from jax.experimental import pallas as pl
from jax.experimental.pallas import tpu as pltpu

F32 = jnp.float32
BF16 = jnp.bfloat16

D_MODEL = 1024
SSD_HEADS = 16
SSD_HEAD_DIM = 64
SSD_GROUPS = 4
SSD_STATE = 128
HEADS_PER_GROUP = SSD_HEADS // SSD_GROUPS
GROUP_WIDTH = HEADS_PER_GROUP * SSD_HEAD_DIM
SSD_INNER = SSD_HEADS * SSD_HEAD_DIM
D_XBC = SSD_INNER + 2 * SSD_GROUPS * SSD_STATE
SSD_CONV = 4
SC_CONV = 3
D_FF = 2816
N_MOD = 9
EPS = 1e-6

LANES = 128
SUBLANES = 8
HALO = SUBLANES
DT_PAD = LANES

C_Z = 0
C_XBC = C_Z + SSD_INNER
C_SCB = C_XBC + D_XBC
C_SCC = C_SCB + D_MODEL
C_SCH = C_SCC + D_MODEL
C_GSSD = C_SCH + D_MODEL
C_GSC = C_GSSD + D_MODEL
C_DT = C_GSC + D_MODEL

MOD_COLS = 2304
FF_CHUNK = 256
FFN_ROWS = 512
MIX_ROWS_PROMPT = 256
MIX_BATCH_SAMPLE = 16
MIX_SUBSTEPS_SAMPLE = 4
SSD_CHUNK = 128
STATE_RING = 4
VMEM_LIMIT = 56 * 1024 * 1024


def _dot(a, b):
    return jnp.dot(a, b, preferred_element_type=F32)


def _dot_nt(a, b):
    return lax.dot_general(a, b, (((1,), (1,)), ((), ())), preferred_element_type=F32)


def _dot_tn(a, b):
    return lax.dot_general(a, b, (((0,), (0,)), ((), ())), preferred_element_type=F32)


def _dot_exact_lhs(sel, x):
    hi = x.astype(BF16)
    r1 = x - hi.astype(F32)
    mid = r1.astype(BF16)
    lo = (r1 - mid.astype(F32)).astype(BF16)
    return _dot(sel, hi) + _dot(sel, mid) + _dot(sel, lo)


def _silu(x):
    return x / (1.0 + jnp.exp(-x))


def _sigmoid(x):
    return 1.0 / (1.0 + jnp.exp(-x))


def _softplus(x):
    return jnp.maximum(x, 0.0) + jnp.log1p(jnp.exp(-jnp.abs(x)))


def _rmsnorm(x, g):
    r = lax.rsqrt(jnp.mean(x * x, axis=-1, keepdims=True) + EPS)
    return (x * r) * g


def _layer_spec(shape, layer):
    nd = len(shape) - 1
    return pl.BlockSpec((1,) + tuple(shape[1:]), lambda *_: (layer,) + (0,) * nd,
                        pipeline_mode=pl.Buffered(1))


def _mod_spec(nb, row0, layer, k):
    assert row0 % nb == 0
    return pl.BlockSpec((1, nb, 1, D_MODEL), lambda b, i: (layer, row0 // nb + b, 0, k))


def _mod_body(c_ref, w_ref, b_ref, o_ref):
    sc = _silu(c_ref[...]).astype(BF16)
    o_ref[0, :, 0, :] = _dot(sc, w_ref[0].astype(BF16)) + b_ref[0]


def _mod_call(c_all, w_ada, b_ada):
    depth, d, n = w_ada.shape
    m = c_all.shape[0]
    tn = MOD_COLS
    return pl.pallas_call(
        _mod_body,
        grid=(depth, n // tn),
        in_specs=[
            pl.BlockSpec((m, d), lambda l, j: (0, 0)),
            pl.BlockSpec((1, d, tn), lambda l, j: (l, 0, j)),
            pl.BlockSpec((1, 1, tn), lambda l, j: (l, 0, j)),
        ],
        out_specs=pl.BlockSpec((1, m, 1, tn), lambda l, j: (l, 0, 0, j)),
        out_shape=jax.ShapeDtypeStruct((depth, m, 1, n), F32),
        compiler_params=pltpu.CompilerParams(
            dimension_semantics=("arbitrary", "arbitrary"),
            vmem_limit_bytes=VMEM_LIMIT),
        name="adaln_mod",
    )(c_all, w_ada, b_ada.reshape(depth, 1, n))


def _ffn_body(x_ref, xn_ref, sh_ref, sc_ref, gt_ref, shn_ref, scn_ref, nw_ref, wgu_ref, wd_ref,
              nf_ref, o_ref, act_ref, hb_ref, act0_ref, *, final_norm):
    nb, lb, d = x_ref.shape

    def normed(xr, shr, scr):
        h = _rmsnorm(xr[...], nw_ref[0]) * (1.0 + scr[0]) + shr[0]
        return h.reshape(nb * lb, d).astype(BF16)

    def act_chunk(hb, c):
        lo = c * FF_CHUNK
        g = _dot(hb, wgu_ref[0, :, lo:lo + FF_CHUNK])
        u = _dot(hb, wgu_ref[0, :, D_FF + lo:D_FF + lo + FF_CHUNK])
        return (_silu(g) * u).astype(BF16)

    @pl.when(jnp.logical_and(pl.program_id(0) == 0, pl.program_id(1) == 0))
    def _():
        hb0 = normed(x_ref, sh_ref, sc_ref)
        hb_ref[...] = hb0
        act0_ref[...] = act_chunk(hb0, 0)

    hb = hb_ref[...]
    act_ref[:, 0:FF_CHUNK] = act0_ref[...]
    n_chunks = D_FF // FF_CHUNK
    for c in range(1, n_chunks):
        act_ref[:, c * FF_CHUNK:(c + 1) * FF_CHUNK] = act_chunk(hb, c)
        if c == n_chunks // 2:
            hb_next = normed(xn_ref, shn_ref, scn_ref)
            hb_ref[...] = hb_next
            act0_ref[...] = act_chunk(hb_next, 0)
    y = _dot(act_ref[...], wd_ref[0])
    out = x_ref[...] + (0.5 * gt_ref[0]) * y.reshape(nb, lb, d)
    if final_norm:
        out = _rmsnorm(out, nf_ref[...])
    o_ref[...] = out


def _ffn_call(x, mod, row0, k_mod, layer, norm_w, w_gu, w_down, norm_final, *, nb, lb,
              final_norm):
    NB, LB, d = x.shape
    rows = nb * lb
    n_tiles = LB // lb
    last = (NB // nb) * n_tiles - 1

    def nxt(b, i):
        flat = jnp.minimum(b * n_tiles + i + 1, last)
        return flat // n_tiles, flat % n_tiles

    def mod_next(k):
        return pl.BlockSpec((1, nb, 1, d), lambda b, i: (layer, row0 // nb + nxt(b, i)[0], 0, k))

    return pl.pallas_call(
        functools.partial(_ffn_body, final_norm=final_norm),
        grid=(NB // nb, LB // lb),
        in_specs=[
            pl.BlockSpec((nb, lb, d), lambda b, i: (b, i, 0)),
            pl.BlockSpec((nb, lb, d), lambda b, i: nxt(b, i) + (0,)),
            _mod_spec(nb, row0, layer, k_mod),
            _mod_spec(nb, row0, layer, k_mod + 1),
            _mod_spec(nb, row0, layer, k_mod + 2),
            mod_next(k_mod), mod_next(k_mod + 1),
            _layer_spec(norm_w.shape, layer),
            _layer_spec(w_gu.shape, layer),
            _layer_spec(w_down.shape, layer),
            pl.BlockSpec((1, d), lambda b, i: (0, 0)),
        ],
        out_specs=pl.BlockSpec((nb, lb, d), lambda b, i: (b, i, 0)),
        out_shape=jax.ShapeDtypeStruct(x.shape, F32),
        scratch_shapes=[pltpu.VMEM((rows, D_FF), BF16), pltpu.VMEM((rows, d), BF16),
                        pltpu.VMEM((rows, FF_CHUNK), BF16)],
        compiler_params=pltpu.CompilerParams(
            dimension_semantics=("arbitrary", "arbitrary"),
            vmem_limit_bytes=VMEM_LIMIT),
        name="ffn",
    )(x, x, mod, mod, mod, mod, mod, norm_w, w_gu, w_down, norm_final.reshape(1, d))


def _causal_conv(ext_ref, cur, w_ref, new_buf_ref, lb, k_w, c0):
    cols = slice(c0, c0 + cur.shape[-1])
    ext_ref[:, HALO:HALO + lb, cols] = cur
    first = HALO - (k_w - 1)
    acc = None
    for k in range(k_w):
        tap = ext_ref[:, first + k:first + k + lb, cols] * w_ref[0, k:k + 1, cols]
        acc = tap if acc is None else acc + tap
    new_buf_ref[0, :, :, cols] = ext_ref[:, lb + first:lb + HALO, cols]
    ext_ref[:, 0:HALO, cols] = ext_ref[:, lb:lb + HALO, cols]
    return acc


def _mixer_body(x_ref, sh_ref, sc_ref, gt_ref, sin_ref, bxin_ref, bsin_ref,
                nw_ref, wzx_ref, wrest_ref, wdt_ref, cw_ref, cb_ref, dtb_ref, alog_ref,
                dexp_ref, snorm_ref, wos_ref, scw_ref, wosc_ref, wo_ref, *rest,
                nb, lb, chunk, nsub, n_alias, state_layer):
    rest = rest[n_alias:]
    (o_ref, sout_ref, bxout_ref, bsout_ref,
     extx_ref, exts_ref, xbc_ref, y_ref, dtda_ref, side_ref) = rest[:10]
    rows = nb * lb
    sub_rows = rows // nsub
    seg = min(lb, chunk)
    nseg = chunk // seg
    seg_shift = seg.bit_length() - 1
    assert seg == 1 << seg_shift and sub_rows % chunk == 0 and rows % nsub == 0
    half = D_MODEL // 2
    single = nsub == 1
    t = pl.program_id(1)
    base = 0 if single else pl.multiple_of(lax.rem(t, nsub) * sub_rows, sub_rows)
    env = {}

    if single:
        @pl.when(t < nsub)
        def _():
            sout_ref[0] = sin_ref[0]
    else:
        sbuf_ref, ssem_ref = rest[10:]
        sb = nb // nsub
        step = pl.program_id(0) * nsub + t
        n_steps = pl.num_programs(0) * nsub

        def state_copy(s):
            slot = lax.rem(s, STATE_RING)
            return pltpu.make_async_copy(
                sin_ref.at[state_layer, pl.ds(s * sb, sb)], sbuf_ref.at[slot], ssem_ref.at[slot])

        @pl.when(step == 0)
        def _():
            for s in range(STATE_RING - 1):
                state_copy(s).start()

        @pl.when(step + STATE_RING - 1 < n_steps)
        def _():
            state_copy(step + STATE_RING - 1).start()

        state_copy(step).wait()
        sout_ref[0] = sbuf_ref[lax.rem(step, STATE_RING)]

    @pl.when(t == 0)
    def _():
        extx_ref[:, HALO - (SSD_CONV - 1):HALO, :] = bxin_ref[0]
        exts_ref[:, HALO - (SC_CONV - 1):HALO, :] = bsin_ref[0]

    def proj(lo, width):
        if lo >= C_DT:
            w_ref, lo = wdt_ref, lo - C_DT
        elif lo >= C_SCB:
            w_ref, lo = wrest_ref, lo - C_SCB
        else:
            w_ref = wzx_ref
        return _dot(env["hb"], w_ref[0, :, lo:lo + width])

    def put_side(k, hf, val):
        if single:
            env["side", k, hf] = val
        else:
            side_ref[:, k * D_MODEL + hf * half:k * D_MODEL + (hf + 1) * half] = val

    def get_side(k):
        if single:
            return jnp.concatenate([env["side", k, 0], env["side", k, 1]], axis=1)
        return side_ref[:, k * D_MODEL:(k + 1) * D_MODEL]

    def rows_of(name, col, r0):
        if single:
            return env[name][r0:r0 + chunk]
        return dtda_ref[pl.ds(base + r0, chunk), col * LANES:(col + 1) * LANES]

    def side_scc(hf):
        env["scc", hf] = proj(C_SCC + hf * half, half)

    def side_v(hf):
        env["v", hf] = env.pop(("scc", hf)) * proj(C_SCH + hf * half, half)

    def side_conv(_):
        v = jnp.concatenate([env.pop(("v", 0)), env.pop(("v", 1))], axis=1)
        env["u"] = _causal_conv(exts_ref, v.reshape(nb, lb, D_MODEL), scw_ref, bsout_ref, lb,
                                SC_CONV, 0).reshape(rows, D_MODEL)

    def side_su(hf):
        lo = hf * half
        env["su", hf] = (proj(C_SCB + lo, half) * env["u"][:, lo:lo + half]).astype(BF16)

    def side_ysc(hf):
        if hf == 0:
            env["su"] = jnp.concatenate([env.pop(("su", 0)), env.pop(("su", 1))], axis=1)
        env["ysc", hf] = _dot(env["su"], wosc_ref[0, :, hf * half:(hf + 1) * half])

    def side_gsc(hf):
        put_side(2, hf, _sigmoid(proj(C_GSC + hf * half, half)) * env.pop(("ysc", hf)))

    def side_gssd(hf):
        put_side(1, hf, _sigmoid(proj(C_GSSD + hf * half, half)))

    def side_z(hf):
        put_side(0, hf, _silu(proj(C_Z + hf * half, half)))

    queue = [functools.partial(f, hf) for f, hf in (
        (side_scc, 0), (side_v, 0), (side_scc, 1), (side_v, 1), (side_conv, 0),
        (side_su, 0), (side_su, 1), (side_ysc, 0), (side_ysc, 1), (side_gsc, 0), (side_gsc, 1),
        (side_gssd, 0), (side_gssd, 1), (side_z, 0), (side_z, 1))]

    def side_step():
        if queue:
            queue.pop(0)()

    ri = lax.broadcasted_iota(jnp.int32, (chunk, chunk), 0)
    ci = lax.broadcasted_iota(jnp.int32, (chunk, chunk), 1)
    same_seq = (ri >> seg_shift) == (ci >> seg_shift)
    causal = jnp.logical_and(same_seq, ci <= ri)
    causal_sel = causal.astype(F32).astype(BF16)
    seq_sel = same_seq.astype(F32).astype(BF16)
    eh = lax.broadcasted_iota(jnp.int32, (LANES, SSD_INNER), 0)
    ej = lax.broadcasted_iota(jnp.int32, (LANES, SSD_INNER), 1)
    head_sel = (eh == ej // SSD_HEAD_DIM).astype(F32).astype(BF16)
    lane_head = lax.broadcasted_iota(jnp.int32, (chunk, GROUP_WIDTH), 1) // SSD_HEAD_DIM
    bc0 = SSD_INNER
    cc0 = SSD_INNER + SSD_GROUPS * SSD_STATE
    n_chunks = sub_rows // chunk

    def normed_input():
        x = x_ref[...]
        h = _rmsnorm(x, nw_ref[0]) * (1.0 + sc_ref[0]) + sh_ref[0]
        env["hb"] = h.reshape(rows, D_MODEL).astype(BF16)

    def dt_proj():
        dt_all = _softplus(proj(C_DT, DT_PAD) + dtb_ref[0])
        da_all = dt_all * (-jnp.exp(alog_ref[0]))
        if single:
            env["dt"], env["da"] = dt_all, da_all
        else:
            dtda_ref[:, 0:LANES] = dt_all
            dtda_ref[:, LANES:2 * LANES] = da_all

    def conv_piece(c0):
        piece = proj(C_XBC + c0, GROUP_WIDTH).reshape(nb, lb, GROUP_WIDTH)
        piece = _causal_conv(extx_ref, piece, cw_ref, bxout_ref, lb, SSD_CONV, c0)
        piece = _silu(piece + cb_ref[0, :, c0:c0 + GROUP_WIDTH])
        xbc_ref[:, c0:c0 + GROUP_WIDTH] = piece.reshape(rows, GROUP_WIDTH)

    def scan_decays(c):
        r0 = c * chunk
        dt = rows_of("dt", 0, r0)
        da = rows_of("da", 1, r0)
        acum = _dot_exact_lhs(causal_sel, da)
        if nseg == 1:
            atot = acum[chunk - 1:chunk, :]
        else:
            atot = _dot_exact_lhs(seq_sel, da)
        ex = _dot(jnp.concatenate([dt * jnp.exp(atot - acum), jnp.exp(acum)],
                                  axis=0).astype(BF16), head_sel)
        return dict(acum=acum, atot=atot, to_end=ex[0:chunk], e_a=ex[chunk:2 * chunk],
                    acum_t=acum.T, dt_t=dt.T)

    def scan_cb(c, st):
        rsl = pl.ds(base + c * chunk, chunk)
        st["bg"] = [xbc_ref[rsl, bc0 + g * SSD_STATE:bc0 + (g + 1) * SSD_STATE].astype(BF16)
                    for g in range(SSD_GROUPS)]
        st["cbm"] = [
            _dot_nt(xbc_ref[rsl, cc0 + g * SSD_STATE:cc0 + (g + 1) * SSD_STATE].astype(BF16),
                    st["bg"][g]) for g in range(SSD_GROUPS)]

    def scan_increments(c, st):
        rsl = pl.ds(base + c * chunk, chunk)
        xs = xbc_ref[rsl, 0:SSD_INNER]
        xd = xs * st["to_end"]
        st["xs_b"] = xs.astype(BF16)
        st["s_add"] = [[
            _dot_tn(xd[j * seg:(j + 1) * seg, g * GROUP_WIDTH:(g + 1) * GROUP_WIDTH].astype(BF16),
                    st["bg"][g][j * seg:(j + 1) * seg]) for j in range(nseg)]
            for g in range(SSD_GROUPS)]

    def scan_outputs(c, st, g):
        r0 = c * chunk
        rsl = pl.ds(base + r0, chunk)
        gl = g * GROUP_WIDTH
        acum, acum_t, dt_t, atot = st["acum"], st["acum_t"], st["dt_t"], st["atot"]
        cg = xbc_ref[rsl, cc0 + g * SSD_STATE:cc0 + (g + 1) * SSD_STATE]
        cbm = st["cbm"][g]
        ms = []
        for r in range(HEADS_PER_GROUP):
            hh = g * HEADS_PER_GROUP + r
            sgm = acum[:, hh:hh + 1] - acum_t[hh:hh + 1, :]
            dec = jnp.exp(jnp.where(causal, sgm, -jnp.inf))
            ms.append((cbm * dec * dt_t[hh:hh + 1, :]).astype(BF16))
        mg = jnp.concatenate(ms, axis=1)
        xg = st["xs_b"][:, gl:gl + GROUP_WIDTH]
        rhs = jnp.concatenate(
            [jnp.where(lane_head == r, xg, jnp.zeros_like(xg))
             for r in range(HEADS_PER_GROUP)], axis=0)
        y_diag = _dot(mg, rhs)

        for j in range(nseg):
            q0 = j * seg
            b_loc = (r0 + q0) // lb
            s_old = sout_ref[0, b_loc, gl:gl + GROUP_WIDTH, :]
            y_off = _dot_nt(cg[q0:q0 + seg].astype(BF16), s_old.astype(BF16))
            y_ref[pl.ds(base + r0 + q0, seg), gl:gl + GROUP_WIDTH] = (
                y_diag[q0:q0 + seg] + y_off * st["e_a"][q0:q0 + seg, gl:gl + GROUP_WIDTH])
            s_add = st["s_add"][g][j]
            q_last = q0 + seg - 1 if nseg > 1 else 0
            for r in range(HEADS_PER_GROUP):
                hh = g * HEADS_PER_GROUP + r
                keep = jnp.exp(atot[q_last:q_last + 1, hh:hh + 1])
                p0 = r * SSD_HEAD_DIM
                sout_ref[0, b_loc, gl + p0:gl + p0 + SSD_HEAD_DIM, :] = (
                    keep * s_old[p0:p0 + SSD_HEAD_DIM] + s_add[p0:p0 + SSD_HEAD_DIM])

    def before_scan():
        normed_input()
        dt_proj()
        for c0 in range(0, D_XBC, GROUP_WIDTH):
            conv_piece(c0)
            side_step()
        while queue:
            side_step()

    def scan():
        for c in range(n_chunks):
            st = scan_decays(c)
            scan_cb(c, st)
            scan_increments(c, st)
            for g in range(SSD_GROUPS):
                scan_outputs(c, st, g)

    def fused_step():
        normed_input()
        for c0 in range(0, D_XBC, GROUP_WIDTH):
            conv_piece(c0)
            side_step()
        dt_proj()
        sts = []
        for c in range(n_chunks):
            sts.append(scan_decays(c))
            scan_cb(c, sts[c])
            scan_increments(c, sts[c])
            side_step()
        for c in range(n_chunks):
            for g in range(SSD_GROUPS):
                scan_outputs(c, sts[c], g)
                side_step()
        after_scan()

    def after_scan():
        while queue:
            side_step()
        y = (y_ref[...] + xbc_ref[:, 0:SSD_INNER] * dexp_ref[0]) * get_side(0)
        parts = []
        for g in range(SSD_GROUPS):
            yg = y[:, g * GROUP_WIDTH:(g + 1) * GROUP_WIDTH]
            parts.append(yg * lax.rsqrt(jnp.mean(yg * yg, axis=-1, keepdims=True) + EPS))
        yn = jnp.concatenate(parts, axis=1) * snorm_ref[0]
        y_ssd = _dot(yn.astype(BF16), wos_ref[0])
        merged = get_side(1) * y_ssd + get_side(2)
        out = _dot(merged.astype(BF16), wo_ref[0]).reshape(nb, lb, D_MODEL)
        o_ref[...] = x_ref[...] + gt_ref[0] * out

    if single:
        fused_step()
    else:
        sub = lax.rem(t, nsub)
        pl.when(sub == 0)(before_scan)
        scan()
        pl.when(sub == nsub - 1)(after_scan)


def _mixer_call(x, mod, row0, layer, state_layer, s_in, bx_in, bs_in, lw, prev, *, nb, lb,
                chunk, nsub):
    NB, LB, d = x.shape
    rows = nb * lb
    sb = nb // nsub
    depth = lw["w_in_zx"].shape[0]
    assert nsub == 1 or LB == lb

    def per_tile(arr, lead):
        return pl.BlockSpec((1, nb) + arr.shape[2:], lambda b, t: (lead, b, 0, 0))

    def per_step(arr, lead):
        return pl.BlockSpec((1, sb) + arr.shape[2:],
                            lambda b, t: (lead, b * nsub + t % nsub, 0, 0))

    consts = [lw["norm_mix"], lw["w_in_zx"], lw["w_in_rest"], lw["w_in_dt"], lw["ssd_conv_w"],
              lw["ssd_conv_b"], lw["dt_bias"], lw["a_log"], lw["d_exp"], lw["ssd_norm"],
              lw["w_out_ssd"], lw["sc_conv_w"], lw["w_out_sc"], lw["w_o"]]
    state_shapes = [(depth,) + a.shape[1:] for a in (s_in, bx_in, bs_in)]
    n_in = 7 + len(consts)
    prev = list(prev) if prev is not None else []
    x_spec = pl.BlockSpec((nb, lb, d), lambda b, t: (b, t // nsub, 0))
    handover_rows = rows if nsub > 1 else SUBLANES
    return pl.pallas_call(
        functools.partial(_mixer_body, nb=nb, lb=lb, chunk=chunk, nsub=nsub, n_alias=len(prev),
                          state_layer=state_layer),
        grid=(NB // nb, (LB // lb) * nsub),
        in_specs=[
            x_spec,
            _mod_spec(nb, row0, layer, 3), _mod_spec(nb, row0, layer, 4),
            _mod_spec(nb, row0, layer, 5),
            per_step(s_in, state_layer) if nsub == 1 else pl.BlockSpec(memory_space=pl.ANY),
            per_tile(bx_in, state_layer),
            per_tile(bs_in, state_layer),
        ] + [_layer_spec(w.shape, layer) for w in consts]
          + [pl.BlockSpec(memory_space=pl.ANY) for _ in prev],
        out_specs=[
            x_spec, per_step(s_in, layer), per_tile(bx_in, layer), per_tile(bs_in, layer),
        ],
        out_shape=[jax.ShapeDtypeStruct(x.shape, F32)]
                  + [jax.ShapeDtypeStruct(s, F32) for s in state_shapes],
        input_output_aliases={n_in + k: 1 + k for k in range(len(prev))},
        scratch_shapes=[
            pltpu.VMEM((nb, HALO + lb, D_XBC), F32),
            pltpu.VMEM((nb, HALO + lb, D_MODEL), F32),
            pltpu.VMEM((rows, D_XBC), F32),
            pltpu.VMEM((rows, SSD_INNER), F32),
            pltpu.VMEM((handover_rows, 2 * LANES), F32),
            pltpu.VMEM((handover_rows, 3 * D_MODEL), F32),
        ] + ([] if nsub == 1 else [
            pltpu.VMEM((STATE_RING, sb) + s_in.shape[2:], F32),
            pltpu.SemaphoreType.DMA((STATE_RING,)),
        ]),
        compiler_params=pltpu.CompilerParams(
            dimension_semantics=("arbitrary", "arbitrary"),
            vmem_limit_bytes=VMEM_LIMIT),
        name="mixer",
    )(x, mod, mod, mod, s_in, bx_in, bs_in, *consts, *prev)


def _mixer_weights(w_in, norm_mix, ssd_conv_w, ssd_conv_b, ssd_dt_bias, ssd_a_log, ssd_d,
                   ssd_norm, w_out_ssd, sc_conv_w, w_out_sc, w_o):
    depth, d, _ = w_in.shape
    dt_lo = SSD_INNER + D_XBC
    dt_hi = dt_lo + SSD_HEADS
    pad = lambda v: jnp.pad(v, ((0, 0), (0, DT_PAD - SSD_HEADS))).reshape(depth, 1, DT_PAD)
    w_b = w_in.astype(BF16)
    return {
        "norm_mix": norm_mix.reshape(depth, 1, d),
        "w_in_zx": w_b[:, :, :dt_lo],
        "w_in_rest": w_b[:, :, dt_hi:],
        "w_in_dt": jnp.pad(w_b[:, :, dt_lo:dt_hi], ((0, 0), (0, 0), (0, DT_PAD - SSD_HEADS))),
        "ssd_conv_w": ssd_conv_w,
        "ssd_conv_b": ssd_conv_b.reshape(depth, 1, D_XBC),
        "dt_bias": pad(ssd_dt_bias),
        "a_log": pad(ssd_a_log),
        "d_exp": jnp.repeat(ssd_d, SSD_HEAD_DIM, axis=1).reshape(depth, 1, SSD_INNER),
        "ssd_norm": ssd_norm.reshape(depth, 1, SSD_INNER),
        "w_out_ssd": w_out_ssd.astype(BF16),
        "sc_conv_w": sc_conv_w,
        "w_out_sc": w_out_sc.astype(BF16),
        "w_o": w_o.astype(BF16),
    }


def kernel(x_prompt, x_sample, c_prompt, c_sample, state_ssm, state_conv_ssd, state_conv_short, w_ada, b_ada, norm_ffn1, norm_mix, norm_ffn2, ffn1_w_gu, ffn1_w_down, ffn2_w_gu, ffn2_w_down, w_in, ssd_conv_w, ssd_conv_b, ssd_dt_bias, ssd_a_log, ssd_d, ssd_norm, w_out_ssd, sc_conv_w, w_out_sc, w_o, norm_final):
    depth = w_ada.shape[0]
    bp, lp, d = x_prompt.shape
    bs, ls, _ = x_sample.shape
    hp = SSD_HEADS * SSD_HEAD_DIM

    mod = _mod_call(jnp.concatenate([c_sample, c_prompt], axis=0), w_ada, b_ada)
    row_s, row_p = 0, bs

    zeros_s = jnp.zeros((1, bp, hp, SSD_STATE), F32)
    zeros_bx = jnp.zeros((1, bp, SSD_CONV - 1, D_XBC), F32)
    zeros_bs = jnp.zeros((1, bp, SC_CONV - 1, D_MODEL), F32)
    state_s = state_ssm.reshape(depth, bs, hp, SSD_STATE)

    ffn_p = dict(nb=1, lb=min(2 * FFN_ROWS, lp))
    ffn_s = dict(nb=min(FFN_ROWS // ls, bs), lb=ls)
    mix_p = dict(nb=1, lb=min(MIX_ROWS_PROMPT, lp), chunk=min(SSD_CHUNK, lp), nsub=1)
    nb_s = min(MIX_BATCH_SAMPLE, bs)
    nsub_s = min(MIX_SUBSTEPS_SAMPLE, nb_s)
    mix_s = dict(nb=nb_s, lb=ls, chunk=nb_s // nsub_s * ls, nsub=nsub_s)

    n1 = norm_ffn1.reshape(depth, 1, d)
    n2 = norm_ffn2.reshape(depth, 1, d)
    gu1, dn1 = ffn1_w_gu.astype(BF16), ffn1_w_down.astype(BF16)
    gu2, dn2 = ffn2_w_gu.astype(BF16), ffn2_w_down.astype(BF16)
    lw = _mixer_weights(w_in, norm_mix, ssd_conv_w, ssd_conv_b, ssd_dt_bias, ssd_a_log, ssd_d,
                        ssd_norm, w_out_ssd, sc_conv_w, w_out_sc, w_o)

    xp, xs = x_prompt, x_sample
    st_p = st_s = None
    for l in range(depth):
        last = l == depth - 1
        xp = _ffn_call(xp, mod, row_p, 0, l, n1, gu1, dn1, norm_final, final_norm=False, **ffn_p)
        xs = _ffn_call(xs, mod, row_s, 0, l, n1, gu1, dn1, norm_final, final_norm=False, **ffn_s)

        xp, *st_p = _mixer_call(xp, mod, row_p, l, 0, zeros_s, zeros_bx, zeros_bs, lw, st_p,
                                **mix_p)
        xs, *st_s = _mixer_call(xs, mod, row_s, l, l, state_s, state_conv_ssd, state_conv_short,
                                lw, st_s, **mix_s)

        xp = _ffn_call(xp, mod, row_p, 6, l, n2, gu2, dn2, norm_final, final_norm=last, **ffn_p)
        xs = _ffn_call(xs, mod, row_s, 6, l, n2, gu2, dn2, norm_final, final_norm=last, **ffn_s)

    shp = (SSD_HEADS, SSD_HEAD_DIM, SSD_STATE)
    return (xp, xs,
            st_p[0].reshape((depth, bp) + shp), st_p[1], st_p[2],
            st_s[0].reshape((depth, bs) + shp), st_s[1], st_s[2])
```
